```python
import math
import jax, jax.numpy as jnp
from jax import lax
import numpy as np

D_MODEL = 1024
BATCH = 8
SEQ = 2048
DEPTH = 4

HEAD_DIM = 64
DIFF_HEADS = 4
DIFF_VDIM = 2 * HEAD_DIM
SB_HEADS = 8
DIFF_WIDTH = DIFF_HEADS * DIFF_VDIM
SB_WIDTH = SB_HEADS * HEAD_DIM
MIX_WIDTH = DIFF_WIDTH + SB_WIDTH
DIFF_QK = DIFF_HEADS * 2 * HEAD_DIM
SB_QK = SB_HEADS * HEAD_DIM
IN_WIDTH = 2 * DIFF_QK + DIFF_WIDTH + 2 * SB_QK + SB_WIDTH
Q_BLOCK = 128

PEER_HEADS = 8
N_KEYS = 128
N_EXPERTS = N_KEYS * N_KEYS
PEER_TOPK = 16
HALF_Q = 128
QUERY_DIM = 2 * HALF_Q
TOKEN_CHUNK = 128

DEEPNORM_ALPHA = (2.0 * DEPTH) ** 0.25
DEEPNORM_BETA = (8.0 * DEPTH) ** -0.25
LN_EPS = 1e-5
RMS_EPS = 1e-5

kernel_name = "hymba_diff_stickbreak_peer_deepnorm"


def layer_norm(h, g, b):
    hf = h.astype(jnp.float32)
    mu = jnp.mean(hf, axis=-1, keepdims=True)
    var = jnp.mean(jnp.square(hf - mu), axis=-1, keepdims=True)
    y = (hf - mu) * lax.rsqrt(var + LN_EPS) * g.astype(jnp.float32) + b.astype(jnp.float32)
    return y.astype(h.dtype)


def alibi_slopes():
    return jnp.asarray([2.0 ** (-8.0 * (i + 1) / DIFF_HEADS) for i in range(DIFF_HEADS)], jnp.float32)


def token_mixer(h, w_in, lq1, lk1, lq2, lk2, subln_g, w_o, layer):
    B, S, _ = h.shape
    proj = h @ w_in
    s1 = DIFF_QK
    s2 = s1 + DIFF_QK
    s3 = s2 + DIFF_WIDTH
    s4 = s3 + SB_QK
    s5 = s4 + SB_QK
    dq, dk, dv, sq, sk, sv = jnp.split(proj, [s1, s2, s3, s4, s5], axis=-1)
    dq = dq.reshape(B, S, DIFF_HEADS, 2, HEAD_DIM).transpose(0, 2, 3, 1, 4)
    dk = dk.reshape(B, S, DIFF_HEADS, 2, HEAD_DIM).transpose(0, 2, 3, 1, 4)
    dv = dv.reshape(B, S, DIFF_HEADS, DIFF_VDIM).transpose(0, 2, 1, 3)
    sq = sq.reshape(B, S, SB_HEADS, HEAD_DIM).transpose(0, 2, 1, 3)
    sk = sk.reshape(B, S, SB_HEADS, HEAD_DIM).transpose(0, 2, 1, 3)
    sv = sv.reshape(B, S, SB_HEADS, HEAD_DIM).transpose(0, 2, 1, 3)

    lambda_init = 0.8 - 0.6 * math.exp(-0.3 * layer)
    lam = (jnp.exp(jnp.sum(lq1.astype(jnp.float32) * lk1.astype(jnp.float32)))
           - jnp.exp(jnp.sum(lq2.astype(jnp.float32) * lk2.astype(jnp.float32)))
           + lambda_init)
    slopes = alibi_slopes()[None, :, None, None, None]
    scale = HEAD_DIM ** -0.5

    diff_out, sb_out = [], []
    for blk in range(S // Q_BLOCK):
        q0 = blk * Q_BLOCK
        kv_len = q0 + Q_BLOCK
        t_pos = q0 + jnp.arange(Q_BLOCK)
        s_pos = jnp.arange(kv_len)
        dist = (t_pos[:, None] - s_pos[None, :]).astype(jnp.float32)

        sc = jnp.einsum('bhmqd,bhmkd->bhmqk', dq[:, :, :, q0:kv_len], dk[:, :, :, :kv_len]).astype(jnp.float32) * scale
        sc = jnp.where(dist >= 0, sc - slopes * dist, -jnp.inf)
        p = jax.nn.softmax(sc, axis=-1)
        attn = p[:, :, 0] - lam * p[:, :, 1]
        diff_out.append(jnp.einsum('bhqk,bhkd->bhqd', attn.astype(dv.dtype), dv[:, :, :kv_len]))

        z = jnp.einsum('bgqd,bgkd->bgqk', sq[:, :, q0:kv_len], sk[:, :, :kv_len]).astype(jnp.float32) * scale
        strict = dist > 0
        log_fail = jnp.where(strict, jax.nn.log_sigmoid(-z), 0.0)
        log_later = lax.cumsum(log_fail, axis=3, reverse=True) - log_fail
        w = jnp.where(strict, jnp.exp(jax.nn.log_sigmoid(z) + log_later), 0.0)
        sb_out.append(jnp.einsum('bgqk,bgkd->bgqd', w.astype(sv.dtype), sv[:, :, :kv_len]))

    diff = jnp.concatenate(diff_out, axis=2).astype(jnp.float32)
    diff = diff * lax.rsqrt(jnp.mean(jnp.square(diff), axis=-1, keepdims=True) + RMS_EPS)
    diff = diff * subln_g.astype(jnp.float32) * (1.0 - lambda_init)
    diff = diff.astype(h.dtype).transpose(0, 2, 1, 3).reshape(B, S, DIFF_WIDTH)
    sb = jnp.concatenate(sb_out, axis=2).transpose(0, 2, 1, 3).reshape(B, S, SB_WIDTH)
    mixed = jnp.concatenate([diff, sb.astype(h.dtype)], axis=-1)
    return mixed @ w_o


def peer_ffn(h, w_query, sub_keys, expert_u, expert_v):
    B, S, D = h.shape
    T = B * S
    xt = h.reshape(T, D)
    q = (xt @ w_query).reshape(T, PEER_HEADS, 2, HALF_Q).astype(jnp.float32)
    scores = jnp.einsum('thpc,hpnc->thpn', q, sub_keys.astype(jnp.float32))
    top_s, top_i = lax.top_k(scores, PEER_TOPK)
    cand_s = top_s[:, :, 0, :, None] + top_s[:, :, 1, None, :]
    cand_i = top_i[:, :, 0, :, None] * N_KEYS + top_i[:, :, 1, None, :]
    cand_s = cand_s.reshape(T, PEER_HEADS, PEER_TOPK * PEER_TOPK)
    cand_i = cand_i.reshape(T, PEER_HEADS, PEER_TOPK * PEER_TOPK)
    best_s, best_pos = lax.top_k(cand_s, PEER_TOPK)
    idx = jnp.take_along_axis(cand_i, best_pos, axis=-1)
    gate = jax.nn.softmax(best_s, axis=-1)
    n_sel = PEER_HEADS * PEER_TOPK
    n_chunks = T // TOKEN_CHUNK

    def expert_chunk(args):
        xc, ic, gc = args
        u = expert_u[ic]
        act = jax.nn.gelu(jnp.einsum('cd,ced->ce', xc, u).astype(jnp.float32), approximate=False)
        v = expert_v[ic]
        return jnp.einsum('ce,ced->cd', (gc * act).astype(v.dtype), v)

    y = lax.map(expert_chunk, (xt.reshape(n_chunks, TOKEN_CHUNK, D),
                               idx.reshape(n_chunks, TOKEN_CHUNK, n_sel),
                               gate.reshape(n_chunks, TOKEN_CHUNK, n_sel)))
    return y.reshape(B, S, D).astype(h.dtype)


def setup_inputs(seed: int = 0) -> dict:
    key = jax.random.key(seed)
    ks = jax.random.split(key, 16)
    f32 = jnp.float32
    beta = DEEPNORM_BETA
    col_scale = np.concatenate([np.ones(2 * DIFF_QK), np.full(DIFF_WIDTH, beta),
                                np.ones(2 * SB_QK), np.full(SB_WIDTH, beta)]).astype(np.float32)
    x = jax.random.normal(ks[0], (BATCH, SEQ, D_MODEL), f32)
    w_in = jax.random.normal(ks[1], (DEPTH, D_MODEL, IN_WIDTH), f32) * (D_MODEL ** -0.5) * jnp.asarray(col_scale)
    lam_q1 = 0.1 * jax.random.normal(ks[2], (DEPTH, HEAD_DIM), f32)
    lam_k1 = 0.1 * jax.random.normal(ks[3], (DEPTH, HEAD_DIM), f32)
    lam_q2 = 0.1 * jax.random.normal(ks[4], (DEPTH, HEAD_DIM), f32)
    lam_k2 = 0.1 * jax.random.normal(ks[5], (DEPTH, HEAD_DIM), f32)
    subln_g = 1.0 + 0.02 * jax.random.normal(ks[6], (DEPTH, DIFF_VDIM), f32)
    w_o = jax.random.normal(ks[7], (DEPTH, MIX_WIDTH, D_MODEL), f32) * (MIX_WIDTH ** -0.5) * beta
    ln1_g = 1.0 + 0.02 * jax.random.normal(ks[8], (DEPTH, D_MODEL), f32)
    ln1_b = 0.02 * jax.random.normal(ks[9], (DEPTH, D_MODEL), f32)
    w_query = jax.random.normal(ks[10], (DEPTH, D_MODEL, PEER_HEADS * QUERY_DIM), f32) * (D_MODEL ** -0.5)
    sub_keys = jax.random.normal(ks[11], (DEPTH, PEER_HEADS, 2, N_KEYS, HALF_Q), f32) * (HALF_Q ** -0.5)
    expert_u = jax.random.normal(ks[12], (DEPTH, N_EXPERTS, D_MODEL), f32) * (D_MODEL ** -0.5) * beta
    expert_v = jax.random.normal(ks[13], (DEPTH, N_EXPERTS, D_MODEL), f32) * beta
    ln2_g = 1.0 + 0.02 * jax.random.normal(ks[14], (DEPTH, D_MODEL), f32)
    ln2_b = 0.02 * jax.random.normal(ks[15], (DEPTH, D_MODEL), f32)
    return {"x": x, "w_in": w_in, "lam_q1": lam_q1, "lam_k1": lam_k1, "lam_q2": lam_q2,
            "lam_k2": lam_k2, "subln_g": subln_g, "w_o": w_o, "ln1_g": ln1_g, "ln1_b": ln1_b,
            "w_query": w_query, "sub_keys": sub_keys, "expert_u": expert_u, "expert_v": expert_v,
            "ln2_g": ln2_g, "ln2_b": ln2_b}


def reference(x, w_in, lam_q1, lam_k1, lam_q2, lam_k2, subln_g, w_o, ln1_g, ln1_b,
              w_query, sub_keys, expert_u, expert_v, ln2_g, ln2_b):
    h = x
    for l in range(DEPTH):
        mix = token_mixer(h, w_in[l], lam_q1[l], lam_k1[l], lam_q2[l], lam_k2[l], subln_g[l], w_o[l], l)
        h = layer_norm(DEEPNORM_ALPHA * h + mix, ln1_g[l], ln1_b[l])
        ffn = peer_ffn(h, w_query[l], sub_keys[l], expert_u[l], expert_v[l])
        h = layer_norm(DEEPNORM_ALPHA * h + ffn, ln2_g[l], ln2_b[l])
    return h
```

```python
import functools
import math

import jax
import jax.numpy as jnp
from jax import lax
from jax.experimental import pallas as pl
from jax.experimental.pallas import tpu as pltpu

F32 = jnp.float32
BF16 = jnp.bfloat16

HEAD_DIM = 64
DIFF_HEADS = 4
SB_HEADS = 8
SLOT = 2 * HEAD_DIM
DIFF_SLOTS = DIFF_HEADS
SB_SLOTS = SB_HEADS // 2
ATT_WIDTH = DIFF_SLOTS * SLOT

PEER_HEADS = 8
N_KEYS = 128
PEER_TOPK = 16
HALF_Q = 128

LN_EPS = 1e-5
RMS_EPS = 1e-5

VMEM_LIMIT = 56 * 1024 * 1024
NEG_INF = float("-inf")


def _params(sem):
    return pltpu.CompilerParams(dimension_semantics=sem, vmem_limit_bytes=VMEM_LIMIT)


def _in_proj_kernel(x_ref, w_ref, o_ref, *, tn):
    x = x_ref[...]
    for n0 in range(0, o_ref.shape[1], tn):
        o_ref[:, n0:n0 + tn] = jnp.dot(x, w_ref[:, n0:n0 + tn], preferred_element_type=F32).astype(o_ref.dtype)


def _in_proj(xb, wb, *, tm=512, tn=512):
    T, K = xb.shape
    N = wb.shape[1]
    return pl.pallas_call(
        functools.partial(_in_proj_kernel, tn=tn),
        grid=(T // tm,),
        in_specs=[pl.BlockSpec((tm, K), lambda i: (i, 0)),
                  pl.BlockSpec((K, N), lambda i: (0, 0))],
        out_specs=pl.BlockSpec((tm, N), lambda i: (i, 0)),
        out_shape=jax.ShapeDtypeStruct((T, N), BF16),
        compiler_params=_params(("parallel",)),
        name="in_proj",
    )(xb, wb)


def _nt_dot(a, b):
    return lax.dot_general(a, b, (((1,), (1,)), ((), ())), preferred_element_type=F32)


def _diff_attn_kernel(sc_ref, slope_ref, q_ref, k_ref, v_ref, g_ref, o_ref, *, blk):
    hd = pl.program_id(1)
    qi = pl.program_id(2)
    lam = sc_ref[0]
    post = sc_ref[1]
    slope = slope_ref[hd]

    lane = lax.broadcasted_iota(jnp.int32, (blk, SLOT), 1)
    q = q_ref[...] * jnp.asarray(HEAD_DIM ** -0.5, BF16)
    zero = jnp.zeros_like(q)
    qm = (jnp.where(lane < HEAD_DIM, q, zero), jnp.where(lane >= HEAD_DIM, q, zero))

    row = lax.broadcasted_iota(jnp.int32, (blk, blk), 0)
    col = lax.broadcasted_iota(jnp.int32, (blk, blk), 1)
    causal = col <= row
    colf = lax.broadcasted_iota(jnp.int32, (1, blk), 1).astype(F32)

    def step(j, carry, masked):
        kb = k_ref[pl.ds(pl.multiple_of(j * blk, blk), blk), :]
        vb = v_ref[pl.ds(pl.multiple_of(j * blk, blk), blk), :]
        bias = slope * (colf + ((j - qi) * blk).astype(F32))
        out = []
        for m in range(2):
            m_old, l_old, a_old = carry[m]
            s = _nt_dot(qm[m], kb) + bias
            if masked:
                s = jnp.where(causal, s, NEG_INF)
            m_new = jnp.maximum(m_old, jnp.max(s, axis=1, keepdims=True))
            alpha = jnp.exp(m_old - m_new)
            p = jnp.exp(s - m_new)
            l_new = alpha * l_old + jnp.sum(p, axis=1, keepdims=True)
            a_new = alpha * a_old + jnp.dot(p.astype(BF16), vb, preferred_element_type=F32)
            out.append((m_new, l_new, a_new))
        return tuple(out)

    init = tuple((jnp.full((blk, 1), NEG_INF, F32), jnp.zeros((blk, 1), F32), jnp.zeros((blk, SLOT), F32))
                 for _ in range(2))
    carry = lax.fori_loop(0, qi, lambda j, c: step(j, c, False), init)
    (_, l1, a1), (_, l2, a2) = step(qi, carry, True)

    d = a1 / l1 - lam * (a2 / l2)
    d = d * lax.rsqrt(jnp.mean(d * d, axis=1, keepdims=True) + RMS_EPS)
    o_ref[...] = (d * g_ref[...] * post).astype(o_ref.dtype)


def _diff_attn(proj, scalars, slopes, subln_g, *, blk=128):
    B, S, _ = proj.shape
    kblk = ATT_WIDTH // SLOT
    return pl.pallas_call(
        functools.partial(_diff_attn_kernel, blk=blk),
        grid=(B, DIFF_SLOTS, S // blk),
        in_specs=[pl.BlockSpec(memory_space=pltpu.SMEM),
                  pl.BlockSpec(memory_space=pltpu.SMEM),
                  pl.BlockSpec((None, blk, SLOT), lambda b, h, i: (b, i, h)),
                  pl.BlockSpec((None, S, SLOT), lambda b, h, i: (b, 0, kblk + h)),
                  pl.BlockSpec((None, S, SLOT), lambda b, h, i: (b, 0, 2 * kblk + h)),
                  pl.BlockSpec((1, SLOT), lambda b, h, i: (0, 0))],
        out_specs=pl.BlockSpec((None, blk, SLOT), lambda b, h, i: (b, i, h)),
        out_shape=jax.ShapeDtypeStruct((B, S, ATT_WIDTH), BF16),
        compiler_params=_params(("parallel", "parallel", "parallel")),
        name="diff_attn",
    )(scalars, slopes, proj, proj, proj, subln_g)


def _sb_attn_kernel(q_ref, k_ref, v_ref, o_ref, *, blk):
    qi = pl.program_id(2)
    lane = lax.broadcasted_iota(jnp.int32, (blk, SLOT), 1)
    q = q_ref[...] * jnp.asarray(HEAD_DIM ** -0.5, BF16)
    zero = jnp.zeros_like(q)

    row = lax.broadcasted_iota(jnp.int32, (blk, blk), 0)
    col = lax.broadcasted_iota(jnp.int32, (blk, blk), 1)
    strict = col < row
    tri = (row > col).astype(BF16)

    def step(j, carry, qg, masked):
        later, acc = carry
        kb = k_ref[pl.ds(pl.multiple_of(j * blk, blk), blk), :]
        vb = v_ref[pl.ds(pl.multiple_of(j * blk, blk), blk), :]
        z = _nt_dot(qg, kb)
        sp = jnp.maximum(z, 0.0) + jnp.log1p(jnp.exp(-jnp.abs(z)))
        lf = -sp
        if masked:
            lf = jnp.where(strict, lf, 0.0)
        hi = lf.astype(BF16)
        lo = (lf - hi.astype(F32)).astype(BF16)
        within = jnp.dot(hi, tri, preferred_element_type=F32) + jnp.dot(lo, tri, preferred_element_type=F32)
        w = jnp.exp((z - sp) + within + later)
        if masked:
            w = jnp.where(strict, w, 0.0)
        acc = acc + jnp.dot(w.astype(BF16), vb, preferred_element_type=F32)
        later = later + jnp.sum(lf, axis=1, keepdims=True)
        return later, acc

    accs = []
    for g in range(2):
        qg = jnp.where((lane >= g * HEAD_DIM) & (lane < (g + 1) * HEAD_DIM), q, zero)
        carry = (jnp.zeros((blk, 1), F32), jnp.zeros((blk, SLOT), F32))
        carry = step(qi, carry, qg, True)
        _, acc = lax.fori_loop(0, qi, lambda jj, c: step(qi - 1 - jj, c, qg, False), carry)
        accs.append(acc)
    o_ref[...] = jnp.where(lane < HEAD_DIM, accs[0], accs[1]).astype(o_ref.dtype)


def _sb_attn(proj, *, blk=128):
    B, S, _ = proj.shape
    kblk = ATT_WIDTH // SLOT
    base = 3 * kblk
    return pl.pallas_call(
        functools.partial(_sb_attn_kernel, blk=blk),
        grid=(B, SB_SLOTS, S // blk),
        in_specs=[pl.BlockSpec((None, blk, SLOT), lambda b, p, i: (b, i, base + p)),
                  pl.BlockSpec((None, S, SLOT), lambda b, p, i: (b, 0, base + kblk + p)),
                  pl.BlockSpec((None, S, SLOT), lambda b, p, i: (b, 0, base + 2 * kblk + p))],
        out_specs=pl.BlockSpec((None, blk, SLOT), lambda b, p, i: (b, i, p)),
        out_shape=jax.ShapeDtypeStruct((B, S, ATT_WIDTH), BF16),
        compiler_params=_params(("parallel", "parallel", "parallel")),
        name="sb_attn",
    )(proj, proj, proj)


def _layer_norm(r, g, b):
    mu = jnp.mean(r, axis=-1, keepdims=True)
    c = r - mu
    var = jnp.mean(c * c, axis=-1, keepdims=True)
    return c * lax.rsqrt(var + LN_EPS) * g + b


def _out_proj_ln_kernel(h_ref, a_ref, b_ref, w_ref, g_ref, beta_ref, o_ref, ob_ref, *, alpha):
    half = a_ref.shape[1]
    y = jnp.dot(a_ref[...], w_ref[:half, :], preferred_element_type=F32)
    y = y + jnp.dot(b_ref[...], w_ref[half:, :], preferred_element_type=F32)
    out = _layer_norm(alpha * h_ref[...] + y, g_ref[...], beta_ref[...])
    o_ref[...] = out
    ob_ref[...] = out.astype(BF16)


def _out_proj_ln(h, a, b, wb, g, beta, *, alpha, tm=512):
    T, D = h.shape
    half = a.shape[1]
    return pl.pallas_call(
        functools.partial(_out_proj_ln_kernel, alpha=alpha),
        grid=(T // tm,),
        in_specs=[pl.BlockSpec((tm, D), lambda i: (i, 0)),
                  pl.BlockSpec((tm, half), lambda i: (i, 0)),
                  pl.BlockSpec((tm, half), lambda i: (i, 0)),
                  pl.BlockSpec((2 * half, D), lambda i: (0, 0)),
                  pl.BlockSpec((1, D), lambda i: (0, 0)),
                  pl.BlockSpec((1, D), lambda i: (0, 0))],
        out_specs=[pl.BlockSpec((tm, D), lambda i: (i, 0)),
                   pl.BlockSpec((tm, D), lambda i: (i, 0))],
        out_shape=[jax.ShapeDtypeStruct((T, D), F32), jax.ShapeDtypeStruct((T, D), BF16)],
        compiler_params=_params(("parallel",)),
        name="out_proj_ln",
    )(h, a, b, wb, g, beta)


def _top16(s):
    work = s
    rank = jnp.full(s.shape, float(PEER_TOPK), F32)
    vals = []
    for k in range(PEER_TOPK):
        m = jnp.max(work, axis=0, keepdims=True)
        eq = work == m
        rank = jnp.where(eq, float(k), rank)
        work = jnp.where(eq, NEG_INF, work)
        vals.append(m)
    return jnp.concatenate(vals, axis=0), rank


def _peer_route_kernel(x_ref, wq_ref, keys_ref, rank2_ref, b_ref, cnt_ref, a_ref):
    tb = x_ref.shape[0]
    qt = _nt_dot(wq_ref[...], x_ref[...]).astype(BF16)
    row8 = lax.broadcasted_iota(jnp.int32, (8, tb), 0)
    for h in range(PEER_HEADS):
        s1 = jnp.dot(keys_ref[2 * h], qt[(2 * h) * HALF_Q:(2 * h + 1) * HALF_Q, :], preferred_element_type=F32)
        s2 = jnp.dot(keys_ref[2 * h + 1], qt[(2 * h + 1) * HALF_Q:(2 * h + 2) * HALF_Q, :],
                     preferred_element_type=F32)
        v1, rank1 = _top16(s1)
        v2, rank2 = _top16(s2)
        pieces = [v1[0:1, :] + v2]
        for k1 in range(1, PEER_TOPK):
            n = PEER_TOPK // (k1 + 1)
            pieces.append(jnp.where(row8 < n, v1[k1:k1 + 1, :] + v2[0:8, :], NEG_INF))
        cand = jnp.concatenate(pieces, axis=0)
        work = cand
        tau = None
        for _ in range(PEER_TOPK):
            tau = jnp.max(work, axis=0, keepdims=True)
            work = jnp.where(work == tau, NEG_INF, work)
        sel = cand >= tau
        top = v1[0:1, :] + v2[0:1, :]
        z = jnp.sum(jnp.where(sel, jnp.exp(cand - top), 0.0), axis=0, keepdims=True)
        self32 = jnp.where(sel, 1.0, 0.0)
        cnt = jnp.zeros(s1.shape, F32)
        off = 0
        for k1 in range(PEER_TOPK):
            rows = PEER_TOPK if k1 == 0 else 8
            c_k1 = jnp.sum(self32[off:off + rows, :], axis=0, keepdims=True)
            off += rows
            cnt = jnp.where(rank1 == float(k1), c_k1, cnt)
        rank2_ref[h] = rank2
        cnt_ref[h] = cnt
        a_ref[h] = jnp.exp(s1 - v1[0:1, :]) / z
        b_ref[h] = jnp.exp(s2 - v2[0:1, :])


def _peer_route(xb, wqt, keys, *, tb=256):
    T, D = xb.shape
    tab = jax.ShapeDtypeStruct((PEER_HEADS, N_KEYS, T), F32)
    tab_spec = pl.BlockSpec((PEER_HEADS, N_KEYS, tb), lambda i: (0, 0, i))
    return pl.pallas_call(
        _peer_route_kernel,
        grid=(T // tb,),
        in_specs=[pl.BlockSpec((tb, D), lambda i: (i, 0)),
                  pl.BlockSpec(wqt.shape, lambda i: (0, 0)),
                  pl.BlockSpec(keys.shape, lambda i: (0, 0, 0))],
        out_specs=[tab_spec] * 4,
        out_shape=[tab] * 4,
        compiler_params=_params(("parallel",)),
        name="peer_route",
    )(xb, wqt, keys)


def _gelu(x):
    return 0.5 * x * (1.0 + lax.erf(x * math.sqrt(0.5)))


def _peer_expert_kernel(h_ref, x_ref, u_ref, vt_ref, rank2_ref, b_ref, cnt_ref, a_ref, g_ref, beta_ref,
                        o_ref, ob_ref, acc_ref, hid_ref, *, alpha):
    e = pl.program_id(1)

    @pl.when(e == 0)
    def _():
        acc_ref[...] = jnp.zeros_like(acc_ref)

    act = _nt_dot(u_ref[...], x_ref[...])
    for ii in range(u_ref.shape[0] // N_KEYS):
        gate = jnp.zeros((N_KEYS, x_ref.shape[0]), F32)
        for h in range(PEER_HEADS):
            sel = rank2_ref[h] < cnt_ref[h, ii:ii + 1, :]
            gate = gate + jnp.where(sel, a_ref[h, ii:ii + 1, :] * b_ref[h], 0.0)
        hid = gate * _gelu(act[ii * N_KEYS:(ii + 1) * N_KEYS, :])
        hid_ref[ii * N_KEYS:(ii + 1) * N_KEYS, :] = hid.astype(BF16)
    acc_ref[...] += jnp.dot(vt_ref[...], hid_ref[...], preferred_element_type=F32)

    @pl.when(e == pl.num_programs(1) - 1)
    def _():
        y = acc_ref[...].T
        out = _layer_norm(alpha * h_ref[...] + y, g_ref[...], beta_ref[...])
        o_ref[...] = out
        ob_ref[...] = out.astype(BF16)


def _peer_expert(h, xb, ub, vtb, tables, g, beta, *, alpha, tb=512, eb=1024):
    T, D = h.shape
    E = ub.shape[0]
    rank2, bexp, cnt, a = tables
    ib = eb // N_KEYS
    full_tab = pl.BlockSpec((PEER_HEADS, N_KEYS, tb), lambda t, e: (0, 0, t))
    row_tab = pl.BlockSpec((PEER_HEADS, ib, tb), lambda t, e: (0, e, t))
    return pl.pallas_call(
        functools.partial(_peer_expert_kernel, alpha=alpha),
        grid=(T // tb, E // eb),
        in_specs=[pl.BlockSpec((tb, D), lambda t, e: (t, 0)),
                  pl.BlockSpec((tb, D), lambda t, e: (t, 0)),
                  pl.BlockSpec((eb, D), lambda t, e: (e, 0)),
                  pl.BlockSpec((D, eb), lambda t, e: (0, e)),
                  full_tab, full_tab, row_tab, row_tab,
                  pl.BlockSpec((1, D), lambda t, e: (0, 0)),
                  pl.BlockSpec((1, D), lambda t, e: (0, 0))],
        out_specs=[pl.BlockSpec((tb, D), lambda t, e: (t, 0)),
                   pl.BlockSpec((tb, D), lambda t, e: (t, 0))],
        out_shape=[jax.ShapeDtypeStruct((T, D), F32), jax.ShapeDtypeStruct((T, D), BF16)],
        scratch_shapes=[pltpu.VMEM((D, tb), F32), pltpu.VMEM((eb, tb), BF16)],
        compiler_params=_params(("parallel", "arbitrary")),
        name="peer_expert",
    )(h, xb, ub, vtb, rank2, bexp, cnt, a, g, beta)


def kernel(x, w_in, lam_q1, lam_k1, lam_q2, lam_k2, subln_g, w_o, ln1_g, ln1_b, w_query, sub_keys, expert_u,
           expert_v, ln2_g, ln2_b):
    B, S, D = x.shape
    T = B * S
    depth = w_in.shape[0]
    alpha = (2.0 * depth) ** 0.25
    slopes = jnp.asarray([2.0 ** (-8.0 * (i + 1) / DIFF_HEADS) for i in range(DIFF_HEADS)], F32)

    h = x.reshape(T, D)
    hb = h.astype(BF16)
    for l in range(depth):
        lambda_init = 0.8 - 0.6 * math.exp(-0.3 * l)
        lam = jnp.exp(jnp.sum(lam_q1[l] * lam_k1[l])) - jnp.exp(jnp.sum(lam_q2[l] * lam_k2[l])) + lambda_init
        scalars = jnp.stack([lam, jnp.asarray(1.0 - lambda_init, F32)]).astype(F32)

        proj = _in_proj(hb, w_in[l].astype(BF16)).reshape(B, S, -1)
        diff = _diff_attn(proj, scalars, slopes, subln_g[l].reshape(1, SLOT))
        sb = _sb_attn(proj)
        h, hb = _out_proj_ln(h, diff.reshape(T, ATT_WIDTH), sb.reshape(T, ATT_WIDTH), w_o[l].astype(BF16),
                             ln1_g[l].reshape(1, D), ln1_b[l].reshape(1, D), alpha=alpha)

        keys = sub_keys[l].reshape(2 * PEER_HEADS, N_KEYS, HALF_Q).astype(BF16)
        tables = _peer_route(hb, w_query[l].T.astype(BF16), keys)
        h, hb = _peer_expert(h, hb, expert_u[l].astype(BF16), expert_v[l].T.astype(BF16), tables,
                             ln2_g[l].reshape(1, D), ln2_b[l].reshape(1, D), alpha=alpha)
    return h.reshape(B, S, D)
```

```python
import functools
import math

import jax
import jax.numpy as jnp
from jax import lax
from jax.experimental import pallas as pl
from jax.experimental.pallas import tpu as pltpu

F32 = jnp.float32
BF16 = jnp.bfloat16

HEAD_DIM = 64
DIFF_HEADS = 4
SB_HEADS = 8
SLOT = 2 * HEAD_DIM
DIFF_SLOTS = DIFF_HEADS
SB_SLOTS = SB_HEADS // 2
ATT_WIDTH = DIFF_SLOTS * SLOT

PEER_HEADS = 8
N_KEYS = 128
PEER_TOPK = 16
HALF_Q = 128

LN_EPS = 1e-5
RMS_EPS = 1e-5

VMEM_LIMIT = 56 * 1024 * 1024
NEG_INF = float("-inf")


def _params(sem):
    return pltpu.CompilerParams(dimension_semantics=sem, vmem_limit_bytes=VMEM_LIMIT)


def _in_proj_kernel(x_ref, w_ref, o_ref, *, tn):
    x = x_ref[...]
    for n0 in range(0, o_ref.shape[1], tn):
        o_ref[:, n0:n0 + tn] = jnp.dot(x, w_ref[:, n0:n0 + tn], preferred_element_type=F32).astype(o_ref.dtype)


def _in_proj(xb, wb, *, tm=512, tn=512):
    T, K = xb.shape
    N = wb.shape[1]
    return pl.pallas_call(
        functools.partial(_in_proj_kernel, tn=tn),
        grid=(T // tm,),
        in_specs=[pl.BlockSpec((tm, K), lambda i: (i, 0)),
                  pl.BlockSpec((K, N), lambda i: (0, 0))],
        out_specs=pl.BlockSpec((tm, N), lambda i: (i, 0)),
        out_shape=jax.ShapeDtypeStruct((T, N), BF16),
        compiler_params=_params(("parallel",)),
        name="in_proj",
    )(xb, wb)


def _nt_dot(a, b):
    return lax.dot_general(a, b, (((1,), (1,)), ((), ())), preferred_element_type=F32)


def _diff_attn_kernel(sc_ref, slope_ref, q_ref, k_ref, v_ref, g_ref, o_ref, *, blk):
    hd = pl.program_id(1)
    qi = pl.program_id(2)
    lam = sc_ref[0]
    post = sc_ref[1]
    slope = slope_ref[hd]

    lane = lax.broadcasted_iota(jnp.int32, (blk, SLOT), 1)
    q = q_ref[...] * jnp.asarray(HEAD_DIM ** -0.5, BF16)
    zero = jnp.zeros_like(q)
    qs = jnp.concatenate([jnp.where(lane < HEAD_DIM, q, zero), jnp.where(lane >= HEAD_DIM, q, zero)], axis=0)

    colf = lax.broadcasted_iota(jnp.int32, (1, blk), 1).astype(F32)
    ones = jnp.ones((blk, SLOT), BF16)

    def step(j, carry, masked):
        m_old, a_old = carry
        kb = k_ref[pl.ds(pl.multiple_of(j * blk, blk), blk), :]
        vb = v_ref[pl.ds(pl.multiple_of(j * blk, blk), blk), :]
        vaug = jnp.concatenate([vb, ones], axis=1)
        bias = slope * (colf + ((j - qi) * blk).astype(F32))
        s = _nt_dot(qs, kb) + bias
        if masked:
            t = lax.broadcasted_iota(jnp.int32, (2 * blk, blk), 0) & (blk - 1)
            s = jnp.where(lax.broadcasted_iota(jnp.int32, (2 * blk, blk), 1) <= t, s, NEG_INF)
        m_new = jnp.maximum(m_old, jnp.max(s, axis=1, keepdims=True))
        alpha = jnp.exp(m_old - m_new)
        p = jnp.exp(s - m_new).astype(BF16)
        return m_new, alpha * a_old + jnp.dot(p, vaug, preferred_element_type=F32)

    init = (jnp.full((2 * blk, 1), NEG_INF, F32), jnp.zeros((2 * blk, 2 * SLOT), F32))
    carry = lax.fori_loop(0, qi, lambda j, c: step(j, c, False), init)
    _, a = step(qi, carry, True)
    a1, a2 = a[:blk], a[blk:]

    d = a1[:, :SLOT] / a1[:, SLOT:] - lam * (a2[:, :SLOT] / a2[:, SLOT:])
    d = d * lax.rsqrt(jnp.mean(d * d, axis=1, keepdims=True) + RMS_EPS)
    o_ref[...] = (d * g_ref[...] * post).astype(o_ref.dtype)


def _diff_attn(proj, scalars, slopes, subln_g, *, blk=512):
    B, S, _ = proj.shape
    kblk = ATT_WIDTH // SLOT
    return pl.pallas_call(
        functools.partial(_diff_attn_kernel, blk=blk),
        grid=(B, DIFF_SLOTS, S // blk),
        in_specs=[pl.BlockSpec(memory_space=pltpu.SMEM),
                  pl.BlockSpec(memory_space=pltpu.SMEM),
                  pl.BlockSpec((None, blk, SLOT), lambda b, h, i: (b, i, h)),
                  pl.BlockSpec((None, S, SLOT), lambda b, h, i: (b, 0, kblk + h)),
                  pl.BlockSpec((None, S, SLOT), lambda b, h, i: (b, 0, 2 * kblk + h)),
                  pl.BlockSpec((1, SLOT), lambda b, h, i: (0, 0))],
        out_specs=pl.BlockSpec((None, blk, SLOT), lambda b, h, i: (b, i, h)),
        out_shape=jax.ShapeDtypeStruct((B, S, ATT_WIDTH), BF16),
        compiler_params=_params(("parallel", "parallel", "parallel")),
        name="diff_attn",
    )(scalars, slopes, proj, proj, proj, subln_g)


def _sb_attn_kernel(q_ref, k_ref, v_ref, o_ref, *, tq, tk):
    qi = pl.program_id(2)
    nsub = tq // tk
    lane = lax.broadcasted_iota(jnp.int32, (tq, SLOT), 1)
    q = q_ref[...] * jnp.asarray(HEAD_DIM ** -0.5, BF16)
    zero = jnp.zeros_like(q)
    qs = jnp.concatenate([jnp.where(lane < HEAD_DIM, q, zero), jnp.where(lane >= HEAD_DIM, q, zero)], axis=0)

    row = lax.broadcasted_iota(jnp.int32, (tk, tk), 0)
    col = lax.broadcasted_iota(jnp.int32, (tk, tk), 1)
    half = jnp.concatenate([(row > col).astype(BF16), jnp.ones((tk, tk), BF16)], axis=1)
    cum_w = jnp.concatenate([half, half], axis=0)

    def chunk(c, carry, masked):
        later, acc = carry
        off = pl.multiple_of(c * tq, tq)
        z = _nt_dot(qs, k_ref[pl.ds(off, tq), :])
        lf = -(jnp.maximum(z, 0.0) + jnp.log(1.0 + jnp.exp(-jnp.abs(z))))
        ls = z + lf
        if masked:
            t = lax.broadcasted_iota(jnp.int32, z.shape, 0) & (tq - 1)
            strict = lax.broadcasted_iota(jnp.int32, z.shape, 1) < t
            lf = jnp.where(strict, lf, 0.0)
        hi = lf.astype(BF16)
        lo = (lf - hi.astype(F32)).astype(BF16)
        sub = [slice(s * tk, (s + 1) * tk) for s in range(nsub)]
        stacked = jnp.concatenate([jnp.concatenate([hi[:, c_], lo[:, c_]], axis=1) for c_ in sub], axis=0)
        cs = jnp.dot(stacked, cum_w, preferred_element_type=F32)
        ws = [None] * nsub
        for s in reversed(range(nsub)):
            cs_s = cs[s * 2 * tq:(s + 1) * 2 * tq]
            w = jnp.exp(ls[:, sub[s]] + cs_s[:, :tk] + later)
            if masked:
                w = jnp.where(strict[:, sub[s]], w, 0.0)
            ws[s] = w.astype(BF16)
            later = later + cs_s[:, tk:]
        acc = acc + jnp.dot(jnp.concatenate(ws, axis=1), v_ref[pl.ds(off, tq), :], preferred_element_type=F32)
        return later, acc

    init = (jnp.zeros((2 * tq, SLOT), F32), jnp.zeros((2 * tq, SLOT), F32))
    carry = chunk(qi, init, True)
    _, acc = lax.fori_loop(0, qi, lambda jj, c: chunk(qi - 1 - jj, c, False), carry)
    o_ref[...] = jnp.where(lane < HEAD_DIM, acc[:tq], acc[tq:]).astype(o_ref.dtype)


def _sb_attn(proj, *, tq=512, tk=128):
    B, S, _ = proj.shape
    kblk = ATT_WIDTH // SLOT
    base = 3 * kblk
    return pl.pallas_call(
        functools.partial(_sb_attn_kernel, tq=tq, tk=tk),
        grid=(B, SB_SLOTS, S // tq),
        in_specs=[pl.BlockSpec((None, tq, SLOT), lambda b, p, i: (b, i, base + p)),
                  pl.BlockSpec((None, S, SLOT), lambda b, p, i: (b, 0, base + kblk + p)),
                  pl.BlockSpec((None, S, SLOT), lambda b, p, i: (b, 0, base + 2 * kblk + p))],
        out_specs=pl.BlockSpec((None, tq, SLOT), lambda b, p, i: (b, i, p)),
        out_shape=jax.ShapeDtypeStruct((B, S, ATT_WIDTH), BF16),
        compiler_params=_params(("parallel", "parallel", "parallel")),
        name="sb_attn",
    )(proj, proj, proj)


def _layer_norm(r, g, b):
    mu = jnp.mean(r, axis=-1, keepdims=True)
    c = r - mu
    var = jnp.mean(c * c, axis=-1, keepdims=True)
    return c * lax.rsqrt(var + LN_EPS) * g + b


def _out_proj_ln_kernel(h_ref, a_ref, b_ref, w_ref, g_ref, beta_ref, o_ref, ob_ref, *, alpha):
    half = a_ref.shape[1]
    y = jnp.dot(a_ref[...], w_ref[:half, :], preferred_element_type=F32)
    y = y + jnp.dot(b_ref[...], w_ref[half:, :], preferred_element_type=F32)
    out = _layer_norm(alpha * h_ref[...] + y, g_ref[...], beta_ref[...])
    o_ref[...] = out
    ob_ref[...] = out.astype(BF16)


def _out_proj_ln(h, a, b, wb, g, beta, *, alpha, tm=512):
    T, D = h.shape
    half = a.shape[1]
    return pl.pallas_call(
        functools.partial(_out_proj_ln_kernel, alpha=alpha),
        grid=(T // tm,),
        in_specs=[pl.BlockSpec((tm, D), lambda i: (i, 0)),
                  pl.BlockSpec((tm, half), lambda i: (i, 0)),
                  pl.BlockSpec((tm, half), lambda i: (i, 0)),
                  pl.BlockSpec((2 * half, D), lambda i: (0, 0)),
                  pl.BlockSpec((1, D), lambda i: (0, 0)),
                  pl.BlockSpec((1, D), lambda i: (0, 0))],
        out_specs=[pl.BlockSpec((tm, D), lambda i: (i, 0)),
                   pl.BlockSpec((tm, D), lambda i: (i, 0))],
        out_shape=[jax.ShapeDtypeStruct((T, D), F32), jax.ShapeDtypeStruct((T, D), BF16)],
        compiler_params=_params(("parallel",)),
        name="out_proj_ln",
    )(h, a, b, wb, g, beta)


def _top16(s):
    work = s
    rank = jnp.full(s.shape, float(PEER_TOPK), F32)
    vals = []
    for k in range(PEER_TOPK):
        m = jnp.max(work, axis=0, keepdims=True)
        eq = work == m
        rank = jnp.where(eq, float(k), rank)
        work = jnp.where(eq, NEG_INF, work)
        vals.append(m)
    return jnp.concatenate(vals, axis=0), rank


def _peer_route_kernel(x_ref, wq_ref, keys_ref, rank2_ref, b_ref, cnt_ref, a_ref):
    tb = x_ref.shape[0]
    qt = _nt_dot(wq_ref[...], x_ref[...]).astype(BF16)
    row8 = lax.broadcasted_iota(jnp.int32, (8, tb), 0)
    for h in range(PEER_HEADS):
        s1 = jnp.dot(keys_ref[2 * h], qt[(2 * h) * HALF_Q:(2 * h + 1) * HALF_Q, :], preferred_element_type=F32)
        s2 = jnp.dot(keys_ref[2 * h + 1], qt[(2 * h + 1) * HALF_Q:(2 * h + 2) * HALF_Q, :],
                     preferred_element_type=F32)
        v1, rank1 = _top16(s1)
        v2, rank2 = _top16(s2)
        pieces = [v1[0:1, :] + v2]
        for k1 in range(1, PEER_TOPK):
            n = PEER_TOPK // (k1 + 1)
            pieces.append(jnp.where(row8 < n, v1[k1:k1 + 1, :] + v2[0:8, :], NEG_INF))
        cand = jnp.concatenate(pieces, axis=0)
        work = cand
        tau = None
        for _ in range(PEER_TOPK):
            tau = jnp.max(work, axis=0, keepdims=True)
            work = jnp.where(work == tau, NEG_INF, work)
        sel = cand >= tau
        top = v1[0:1, :] + v2[0:1, :]
        z = jnp.sum(jnp.where(sel, jnp.exp(cand - top), 0.0), axis=0, keepdims=True)
        self32 = jnp.where(sel, 1.0, 0.0)
        cnt = jnp.zeros(s1.shape, F32)
        off = 0
        for k1 in range(PEER_TOPK):
            rows = PEER_TOPK if k1 == 0 else 8
            c_k1 = jnp.sum(self32[off:off + rows, :], axis=0, keepdims=True)
            off += rows
            cnt = jnp.where(rank1 == float(k1), c_k1, cnt)
        rank2_ref[h] = rank2.astype(BF16)
        b_ref[h] = jnp.exp(s2 - v2[0:1, :]).astype(BF16)
        cnt_ref[h] = cnt
        a_ref[h] = jnp.exp(s1 - v1[0:1, :]) * (0.5 / z)


def _peer_route(xb, wqt, keys, *, tb=256):
    T, D = xb.shape
    tab_spec = pl.BlockSpec((PEER_HEADS, N_KEYS, tb), lambda i: (0, 0, i))
    return pl.pallas_call(
        _peer_route_kernel,
        grid=(T // tb,),
        in_specs=[pl.BlockSpec((tb, D), lambda i: (i, 0)),
                  pl.BlockSpec(wqt.shape, lambda i: (0, 0)),
                  pl.BlockSpec(keys.shape, lambda i: (0, 0, 0))],
        out_specs=[tab_spec] * 4,
        out_shape=[jax.ShapeDtypeStruct((PEER_HEADS, N_KEYS, T), dt) for dt in (BF16, BF16, F32, F32)],
        compiler_params=_params(("parallel",)),
        name="peer_route",
    )(xb, wqt, keys)


def _gelu(x):
    return 0.5 * x * (1.0 + lax.erf(x * math.sqrt(0.5)))


def _peer_expert_kernel(h_ref, x_ref, u_ref, vt_ref, rank2_ref, b_ref, cnt_ref, a_ref, g_ref, beta_ref,
                        o_ref, ob_ref, acc_ref, hid_ref, *, alpha):
    e = pl.program_id(1)

    @pl.when(e == 0)
    def _():
        acc_ref[...] = jnp.zeros_like(acc_ref)

    act = _nt_dot(u_ref[...], x_ref[...])
    tb = x_ref.shape[0]
    pack = 16
    zero = jnp.zeros((N_KEYS // pack, pack, tb), BF16)

    def row(ref, h, ii):
        return jnp.broadcast_to(ref[h, ii:ii + 1, :], (pack, tb)).astype(BF16)[None]

    for ii in range(u_ref.shape[0] // N_KEYS):
        gate = zero
        for h in range(PEER_HEADS):
            sel = rank2_ref[h].reshape(zero.shape) < row(cnt_ref, h, ii)
            gate = gate + jnp.where(sel, b_ref[h].reshape(zero.shape), zero) * row(a_ref, h, ii)
        a = act[ii * N_KEYS:(ii + 1) * N_KEYS, :]
        hid = gate.reshape(N_KEYS, tb) * (a * (1.0 + lax.erf(a * math.sqrt(0.5)))).astype(BF16)
        hid_ref[ii * N_KEYS:(ii + 1) * N_KEYS, :] = hid
    acc_ref[...] += jnp.dot(vt_ref[...], hid_ref[...], preferred_element_type=F32)

    @pl.when(e == pl.num_programs(1) - 1)
    def _():
        y = acc_ref[...].T
        out = _layer_norm(alpha * h_ref[...] + y, g_ref[...], beta_ref[...])
        o_ref[...] = out
        ob_ref[...] = out.astype(BF16)


def _peer_expert(h, xb, ub, vtb, tables, g, beta, *, alpha, tb=512, eb=2048):
    T, D = h.shape
    E = ub.shape[0]
    rank2, bexp, cnt, a = tables
    ib = eb // N_KEYS
    full_tab = pl.BlockSpec((PEER_HEADS, N_KEYS, tb), lambda t, e: (0, 0, t))
    row_tab = pl.BlockSpec((PEER_HEADS, ib, tb), lambda t, e: (0, e, t))
    return pl.pallas_call(
        functools.partial(_peer_expert_kernel, alpha=alpha),
        grid=(T // tb, E // eb),
        in_specs=[pl.BlockSpec((tb, D), lambda t, e: (t, 0)),
                  pl.BlockSpec((tb, D), lambda t, e: (t, 0)),
                  pl.BlockSpec((eb, D), lambda t, e: (e, 0)),
                  pl.BlockSpec((D, eb), lambda t, e: (0, e)),
                  full_tab, full_tab, row_tab, row_tab,
                  pl.BlockSpec((1, D), lambda t, e: (0, 0)),
                  pl.BlockSpec((1, D), lambda t, e: (0, 0))],
        out_specs=[pl.BlockSpec((tb, D), lambda t, e: (t, 0)),
                   pl.BlockSpec((tb, D), lambda t, e: (t, 0))],
        out_shape=[jax.ShapeDtypeStruct((T, D), F32), jax.ShapeDtypeStruct((T, D), BF16)],
        scratch_shapes=[pltpu.VMEM((D, tb), F32), pltpu.VMEM((eb, tb), BF16)],
        compiler_params=_params(("parallel", "arbitrary")),
        name="peer_expert",
    )(h, xb, ub, vtb, rank2, bexp, cnt, a, g, beta)


def kernel(x, w_in, lam_q1, lam_k1, lam_q2, lam_k2, subln_g, w_o, ln1_g, ln1_b, w_query, sub_keys, expert_u,
           expert_v, ln2_g, ln2_b):
    B, S, D = x.shape
    T = B * S
    depth = w_in.shape[0]
    alpha = (2.0 * depth) ** 0.25
    slopes = jnp.asarray([2.0 ** (-8.0 * (i + 1) / DIFF_HEADS) for i in range(DIFF_HEADS)], F32)

    h = x.reshape(T, D)
    hb = h.astype(BF16)
    for l in range(depth):
        lambda_init = 0.8 - 0.6 * math.exp(-0.3 * l)
        lam = jnp.exp(jnp.sum(lam_q1[l] * lam_k1[l])) - jnp.exp(jnp.sum(lam_q2[l] * lam_k2[l])) + lambda_init
        scalars = jnp.stack([lam, jnp.asarray(1.0 - lambda_init, F32)]).astype(F32)

        proj = _in_proj(hb, w_in[l].astype(BF16)).reshape(B, S, -1)
        diff = _diff_attn(proj, scalars, slopes, subln_g[l].reshape(1, SLOT))
        sb = _sb_attn(proj)
        h, hb = _out_proj_ln(h, diff.reshape(T, ATT_WIDTH), sb.reshape(T, ATT_WIDTH), w_o[l].astype(BF16),
                             ln1_g[l].reshape(1, D), ln1_b[l].reshape(1, D), alpha=alpha)

        keys = sub_keys[l].reshape(2 * PEER_HEADS, N_KEYS, HALF_Q).astype(BF16)
        tables = _peer_route(hb, w_query[l].T.astype(BF16), keys)
        h, hb = _peer_expert(h, hb, expert_u[l].astype(BF16), expert_v[l].T.astype(BF16), tables,
                             ln2_g[l].reshape(1, D), ln2_b[l].reshape(1, D), alpha=alpha)
    return h.reshape(B, S, D)
```

```python
import functools
import math

import jax
import jax.numpy as jnp
from jax import lax
from jax.experimental import pallas as pl
from jax.experimental.pallas import tpu as pltpu

F32 = jnp.float32
BF16 = jnp.bfloat16

HEAD_DIM = 64
DIFF_HEADS = 4
SB_HEADS = 8
SLOT = 2 * HEAD_DIM
DIFF_SLOTS = DIFF_HEADS
SB_SLOTS = SB_HEADS // 2
ATT_WIDTH = DIFF_SLOTS * SLOT
QUERY_GROUPS = (0, 3)
LOG2E = math.log2(math.e)
QUERY_SCALE = HEAD_DIM ** -0.5 * LOG2E

PEER_HEADS = 8
N_KEYS = 128
PEER_TOPK = 16
HALF_Q = 128

LN_EPS = 1e-5
RMS_EPS = 1e-5

VMEM_LIMIT = 56 * 1024 * 1024
NEG_INF = float("-inf")


def _params(sem):
    return pltpu.CompilerParams(dimension_semantics=sem, vmem_limit_bytes=VMEM_LIMIT)


def _in_proj_kernel(x_ref, w_ref, o_ref, *, tn):
    x = x_ref[...]
    for n0 in range(0, o_ref.shape[1], tn):
        y = jnp.dot(x, w_ref[:, n0:n0 + tn], preferred_element_type=F32)
        if n0 // ATT_WIDTH in QUERY_GROUPS:
            y = y * QUERY_SCALE
        o_ref[:, n0:n0 + tn] = y.astype(o_ref.dtype)


def _in_proj(xb, wb, *, tm=512, tn=ATT_WIDTH):
    T, K = xb.shape
    N = wb.shape[1]
    return pl.pallas_call(
        functools.partial(_in_proj_kernel, tn=tn),
        grid=(T // tm,),
        in_specs=[pl.BlockSpec((tm, K), lambda i: (i, 0)),
                  pl.BlockSpec((K, N), lambda i: (0, 0))],
        out_specs=pl.BlockSpec((tm, N), lambda i: (i, 0)),
        out_shape=jax.ShapeDtypeStruct((T, N), BF16),
        compiler_params=_params(("parallel",)),
        name="in_proj",
    )(xb, wb)


def _nt_dot(a, b):
    return lax.dot_general(a, b, (((1,), (1,)), ((), ())), preferred_element_type=F32)


def _diff_attn_kernel(sc_ref, slope_ref, q_ref, k_ref, v_ref, g_ref, o_ref, *, blk):
    hd = pl.program_id(1)
    qi = pl.program_id(2)
    lam = sc_ref[0]
    post = sc_ref[1]
    slope = slope_ref[hd]

    lane = lax.broadcasted_iota(jnp.int32, (blk, SLOT), 1)
    q = q_ref[...]
    zero = jnp.zeros_like(q)
    qs = jnp.concatenate([jnp.where(lane < HEAD_DIM, q, zero), jnp.where(lane >= HEAD_DIM, q, zero)], axis=0)

    colf = lax.broadcasted_iota(jnp.int32, (1, blk), 1).astype(F32)
    ones = jnp.ones((blk, SLOT), BF16)

    def step(j, carry, masked):
        m_old, a_old = carry
        kb = k_ref[pl.ds(pl.multiple_of(j * blk, blk), blk), :]
        vb = v_ref[pl.ds(pl.multiple_of(j * blk, blk), blk), :]
        vaug = jnp.concatenate([vb, ones], axis=1)
        bias = (slope * LOG2E) * (colf + ((j - qi) * blk).astype(F32))
        s = _nt_dot(qs, kb) + bias
        if masked:
            t = lax.broadcasted_iota(jnp.int32, (2 * blk, blk), 0) & (blk - 1)
            s = jnp.where(lax.broadcasted_iota(jnp.int32, (2 * blk, blk), 1) <= t, s, NEG_INF)
        m_new = jnp.maximum(m_old, jnp.max(s, axis=1, keepdims=True))
        alpha = jnp.exp2(m_old - m_new)
        p = jnp.exp2(s - m_new).astype(BF16)
        return m_new, alpha * a_old + jnp.dot(p, vaug, preferred_element_type=F32)

    init = (jnp.full((2 * blk, 1), NEG_INF, F32), jnp.zeros((2 * blk, 2 * SLOT), F32))
    carry = lax.fori_loop(0, qi, lambda j, c: step(j, c, False), init)
    _, a = step(qi, carry, True)
    a1, a2 = a[:blk], a[blk:]

    d = a1[:, :SLOT] / a1[:, SLOT:] - lam * (a2[:, :SLOT] / a2[:, SLOT:])
    d = d * lax.rsqrt(jnp.mean(d * d, axis=1, keepdims=True) + RMS_EPS)
    o_ref[...] = (d * g_ref[...] * post).astype(o_ref.dtype)


def _diff_attn(proj, scalars, slopes, subln_g, *, blk=512):
    B, S, _ = proj.shape
    kblk = ATT_WIDTH // SLOT
    return pl.pallas_call(
        functools.partial(_diff_attn_kernel, blk=blk),
        grid=(B, DIFF_SLOTS, S // blk),
        in_specs=[pl.BlockSpec(memory_space=pltpu.SMEM),
                  pl.BlockSpec(memory_space=pltpu.SMEM),
                  pl.BlockSpec((None, blk, SLOT), lambda b, h, i: (b, i, h)),
                  pl.BlockSpec((None, S, SLOT), lambda b, h, i: (b, 0, kblk + h)),
                  pl.BlockSpec((None, S, SLOT), lambda b, h, i: (b, 0, 2 * kblk + h)),
                  pl.BlockSpec((1, SLOT), lambda b, h, i: (0, 0))],
        out_specs=pl.BlockSpec((None, blk, SLOT), lambda b, h, i: (b, i, h)),
        out_shape=jax.ShapeDtypeStruct((B, S, ATT_WIDTH), BF16),
        compiler_params=_params(("parallel", "parallel", "parallel")),
        name="diff_attn",
    )(scalars, slopes, proj, proj, proj, subln_g)


def _sb_attn_kernel(q_ref, k_ref, v_ref, o_ref, *, tq, tk):
    qi = pl.program_id(2)
    nsub = tq // tk
    lane = lax.broadcasted_iota(jnp.int32, (tq, SLOT), 1)
    q = q_ref[...]
    zero = jnp.zeros_like(q)
    qs = jnp.concatenate([jnp.where(lane < HEAD_DIM, q, zero), jnp.where(lane >= HEAD_DIM, q, zero)], axis=0)

    row = lax.broadcasted_iota(jnp.int32, (tk, tk), 0)
    col = lax.broadcasted_iota(jnp.int32, (tk, tk), 1)
    half = jnp.concatenate([jnp.where(row > col, -1.0, 0.0).astype(BF16), jnp.full((tk, tk), -1.0, BF16)], axis=1)
    cum_w = jnp.concatenate([half, half], axis=0)
    sign_bit = jnp.int32(-2 ** 31)

    def chunk(c, carry, masked):
        later, acc = carry
        off = pl.multiple_of(c * tq, tq)
        z = _nt_dot(qs, k_ref[pl.ds(off, tq), :])
        neg_abs = lax.bitcast_convert_type(lax.bitcast_convert_type(z, jnp.int32) | sign_bit, F32)
        sp = jnp.maximum(z, 0.0) + LOG2E * jnp.log(1.0 + jnp.exp2(neg_abs))
        ls = z - sp
        lf = sp
        if masked:
            t = lax.broadcasted_iota(jnp.int32, z.shape, 0) & (tq - 1)
            strict = lax.broadcasted_iota(jnp.int32, z.shape, 1) < t
            lf = jnp.where(strict, lf, 0.0)
        hi = lf.astype(BF16)
        lo = (lf - hi.astype(F32)).astype(BF16)
        sub = [slice(s * tk, (s + 1) * tk) for s in range(nsub)]
        stacked = jnp.concatenate([jnp.concatenate([hi[:, c_], lo[:, c_]], axis=1) for c_ in sub], axis=0)
        cs = jnp.dot(stacked, cum_w, preferred_element_type=F32)
        ws = [None] * nsub
        for s in reversed(range(nsub)):
            cs_s = cs[s * 2 * tq:(s + 1) * 2 * tq]
            w = jnp.exp2(ls[:, sub[s]] + cs_s[:, :tk] + later)
            if masked:
                w = jnp.where(strict[:, sub[s]], w, 0.0)
            ws[s] = w.astype(BF16)
            later = later + cs_s[:, tk:]
        acc = acc + jnp.dot(jnp.concatenate(ws, axis=1), v_ref[pl.ds(off, tq), :], preferred_element_type=F32)
        return later, acc

    init = (jnp.zeros((2 * tq, SLOT), F32), jnp.zeros((2 * tq, SLOT), F32))
    carry = chunk(qi, init, True)
    _, acc = lax.fori_loop(0, qi, lambda jj, c: chunk(qi - 1 - jj, c, False), carry)
    o_ref[...] = jnp.where(lane < HEAD_DIM, acc[:tq], acc[tq:]).astype(o_ref.dtype)


def _sb_attn(proj, *, tq=512, tk=128):
    B, S, _ = proj.shape
    kblk = ATT_WIDTH // SLOT
    base = 3 * kblk
    return pl.pallas_call(
        functools.partial(_sb_attn_kernel, tq=tq, tk=tk),
        grid=(B, SB_SLOTS, S // tq),
        in_specs=[pl.BlockSpec((None, tq, SLOT), lambda b, p, i: (b, i, base + p)),
                  pl.BlockSpec((None, S, SLOT), lambda b, p, i: (b, 0, base + kblk + p)),
                  pl.BlockSpec((None, S, SLOT), lambda b, p, i: (b, 0, base + 2 * kblk + p))],
        out_specs=pl.BlockSpec((None, tq, SLOT), lambda b, p, i: (b, i, p)),
        out_shape=jax.ShapeDtypeStruct((B, S, ATT_WIDTH), BF16),
        compiler_params=_params(("parallel", "parallel", "parallel")),
        name="sb_attn",
    )(proj, proj, proj)


def _layer_norm(r, g, b):
    mu = jnp.mean(r, axis=-1, keepdims=True)
    c = r - mu
    var = jnp.mean(c * c, axis=-1, keepdims=True)
    return c * lax.rsqrt(var + LN_EPS) * g + b


def _out_proj_ln_kernel(h_ref, a_ref, b_ref, w_ref, g_ref, beta_ref, o_ref, ob_ref, *, alpha):
    half = a_ref.shape[1]
    y = jnp.dot(a_ref[...], w_ref[:half, :], preferred_element_type=F32)
    y = y + jnp.dot(b_ref[...], w_ref[half:, :], preferred_element_type=F32)
    out = _layer_norm(alpha * h_ref[...] + y, g_ref[...], beta_ref[...])
    o_ref[...] = out
    ob_ref[...] = out.astype(BF16)


def _out_proj_ln(h, a, b, wb, g, beta, *, alpha, tm=512):
    T, D = h.shape
    half = a.shape[1]
    return pl.pallas_call(
        functools.partial(_out_proj_ln_kernel, alpha=alpha),
        grid=(T // tm,),
        in_specs=[pl.BlockSpec((tm, D), lambda i: (i, 0)),
                  pl.BlockSpec((tm, half), lambda i: (i, 0)),
                  pl.BlockSpec((tm, half), lambda i: (i, 0)),
                  pl.BlockSpec((2 * half, D), lambda i: (0, 0)),
                  pl.BlockSpec((1, D), lambda i: (0, 0)),
                  pl.BlockSpec((1, D), lambda i: (0, 0))],
        out_specs=[pl.BlockSpec((tm, D), lambda i: (i, 0)),
                   pl.BlockSpec((tm, D), lambda i: (i, 0))],
        out_shape=[jax.ShapeDtypeStruct((T, D), F32), jax.ShapeDtypeStruct((T, D), BF16)],
        compiler_params=_params(("parallel",)),
        name="out_proj_ln",
    )(h, a, b, wb, g, beta)


def _top16(s, with_rank):
    work = s
    rank = jnp.full(s.shape, float(PEER_TOPK), F32) if with_rank else None
    vals = []
    for k in range(PEER_TOPK):
        m = jnp.max(work, axis=0, keepdims=True)
        eq = work == m
        if with_rank:
            rank = jnp.where(eq, float(k), rank)
        work = jnp.where(eq, NEG_INF, work)
        vals.append(m)
    return jnp.concatenate(vals, axis=0), rank


def _peer_route_kernel(x_ref, wq_ref, keys_ref, rank2_ref, b_ref, cnt_ref, a_ref):
    tb = x_ref.shape[0]
    qt = _nt_dot(wq_ref[...], x_ref[...]).astype(BF16)
    row8 = lax.broadcasted_iota(jnp.int32, (8, tb), 0)
    row16 = lax.broadcasted_iota(jnp.int32, (PEER_TOPK, tb), 0)
    for h in range(PEER_HEADS):
        s1 = jnp.dot(keys_ref[2 * h], qt[(2 * h) * HALF_Q:(2 * h + 1) * HALF_Q, :], preferred_element_type=F32)
        s2 = jnp.dot(keys_ref[2 * h + 1], qt[(2 * h + 1) * HALF_Q:(2 * h + 2) * HALF_Q, :],
                     preferred_element_type=F32)
        v1, _ = _top16(s1, False)
        v2, rank2 = _top16(s2, True)
        v2lo = v2[0:8, :]
        pieces = [v1 + v2[0:1, :],
                  v1[0:8, :] + v2[1:2, :],
                  jnp.where(row16 >= 2, v1[0:1, :] + v2, NEG_INF),
                  jnp.where(row8 >= 2, v1[1:2, :] + v2lo, NEG_INF),
                  jnp.where((row8 >= 2) & (row8 <= 4), v1[2:3, :] + v2lo, NEG_INF),
                  jnp.where((row8 >= 2) & (row8 <= 3), v1[3:4, :] + v2lo, NEG_INF),
                  jnp.where(row8 == 2, v1[4:5, :] + v2lo, NEG_INF)]
        cand = jnp.concatenate(pieces, axis=0)
        work = cand
        tau = None
        for _ in range(PEER_TOPK):
            tau = jnp.max(work, axis=0, keepdims=True)
            work = jnp.where(work == tau, NEG_INF, work)
        sel = cand >= tau
        top = v1[0:1, :] + v2[0:1, :]
        z = jnp.sum(jnp.where(sel, jnp.exp(cand - top), 0.0), axis=0, keepdims=True)
        self32 = jnp.where(sel, 1.0, 0.0)
        per_row = self32[0:16] + jnp.concatenate([self32[16:24], jnp.zeros((8, tb), F32)], axis=0)
        off = 24
        for k1, rows in enumerate((16, 8, 8, 8, 8)):
            per_row = per_row + jnp.where(row16 == k1, jnp.sum(self32[off:off + rows], axis=0, keepdims=True), 0.0)
            off += rows
        cnt = jnp.zeros(s1.shape, F32)
        for k1 in range(PEER_TOPK):
            cnt = jnp.where(s1 == v1[k1:k1 + 1, :], per_row[k1:k1 + 1, :], cnt)
        rank2_ref[h] = rank2.astype(BF16)
        b_ref[h] = jnp.exp(s2 - v2[0:1, :]).astype(BF16)
        cnt_ref[h] = cnt
        a_ref[h] = jnp.exp(s1 - v1[0:1, :]) * (0.5 / z)


def _peer_route(xb, wqt, keys, *, tb=256):
    T, D = xb.shape
    tab_spec = pl.BlockSpec((PEER_HEADS, N_KEYS, tb), lambda i: (0, 0, i))
    return pl.pallas_call(
        _peer_route_kernel,
        grid=(T // tb,),
        in_specs=[pl.BlockSpec((tb, D), lambda i: (i, 0)),
                  pl.BlockSpec(wqt.shape, lambda i: (0, 0)),
                  pl.BlockSpec(keys.shape, lambda i: (0, 0, 0))],
        out_specs=[tab_spec] * 4,
        out_shape=[jax.ShapeDtypeStruct((PEER_HEADS, N_KEYS, T), dt) for dt in (BF16, BF16, F32, F32)],
        compiler_params=_params(("parallel",)),
        name="peer_route",
    )(xb, wqt, keys)


def _peer_expert_kernel(h_ref, x_ref, u_ref, vt_ref, rank2_ref, b_ref, cnt_ref, a_ref, g_ref, beta_ref,
                        o_ref, ob_ref, acc_ref, hid_ref, *, alpha):
    e = pl.program_id(1)

    @pl.when(e == 0)
    def _():
        acc_ref[...] = jnp.zeros_like(acc_ref)

    tb = x_ref.shape[0]
    pack = 16
    zero = jnp.zeros((N_KEYS // pack, pack, tb), BF16)

    def row(ref, h, ii):
        return jnp.broadcast_to(ref[h, ii:ii + 1, :], (pack, tb)).astype(BF16)[None]

    act = _nt_dot(u_ref[...], x_ref[...])
    for ii in range(u_ref.shape[0] // N_KEYS):
        gate = zero
        for h in range(PEER_HEADS):
            sel = rank2_ref[h].reshape(zero.shape) < row(cnt_ref, h, ii)
            gate = gate + jnp.where(sel, b_ref[h].reshape(zero.shape), zero) * row(a_ref, h, ii)
        a = act[ii * N_KEYS:(ii + 1) * N_KEYS, :]
        hid = gate.reshape(N_KEYS, tb) * (a * (1.0 + lax.erf(a * math.sqrt(0.5)))).astype(BF16)
        hid_ref[ii * N_KEYS:(ii + 1) * N_KEYS, :] = hid
    acc_ref[...] += jnp.dot(vt_ref[...], hid_ref[...], preferred_element_type=F32)

    @pl.when(e == pl.num_programs(1) - 1)
    def _():
        y = acc_ref[...].T
        out = _layer_norm(alpha * h_ref[...] + y, g_ref[...], beta_ref[...])
        o_ref[...] = out
        ob_ref[...] = out.astype(BF16)


def _peer_expert(h, xb, ub, vtb, tables, g, beta, *, alpha, tb=512, eb=2048):
    T, D = h.shape
    E = ub.shape[0]
    rank2, bexp, cnt, a = tables
    n_i = eb // N_KEYS
    full_tab = pl.BlockSpec((PEER_HEADS, N_KEYS, tb), lambda t, e: (0, 0, t))
    row_tab = pl.BlockSpec((PEER_HEADS, n_i, tb), lambda t, e: (0, e, t))
    return pl.pallas_call(
        functools.partial(_peer_expert_kernel, alpha=alpha),
        grid=(T // tb, E // eb),
        in_specs=[pl.BlockSpec((tb, D), lambda t, e: (t, 0)),
                  pl.BlockSpec((tb, D), lambda t, e: (t, 0)),
                  pl.BlockSpec((eb, D), lambda t, e: (e, 0)),
                  pl.BlockSpec((D, eb), lambda t, e: (0, e)),
                  full_tab, full_tab, row_tab, row_tab,
                  pl.BlockSpec((1, D), lambda t, e: (0, 0)),
                  pl.BlockSpec((1, D), lambda t, e: (0, 0))],
        out_specs=[pl.BlockSpec((tb, D), lambda t, e: (t, 0)),
                   pl.BlockSpec((tb, D), lambda t, e: (t, 0))],
        out_shape=[jax.ShapeDtypeStruct((T, D), F32), jax.ShapeDtypeStruct((T, D), BF16)],
        scratch_shapes=[pltpu.VMEM((D, tb), F32), pltpu.VMEM((eb, tb), BF16)],
        compiler_params=_params(("parallel", "arbitrary")),
        name="peer_expert",
    )(h, xb, ub, vtb, rank2, bexp, cnt, a, g, beta)


def kernel(x, w_in, lam_q1, lam_k1, lam_q2, lam_k2, subln_g, w_o, ln1_g, ln1_b, w_query, sub_keys, expert_u,
           expert_v, ln2_g, ln2_b):
    B, S, D = x.shape
    T = B * S
    depth = w_in.shape[0]
    alpha = (2.0 * depth) ** 0.25
    slopes = jnp.asarray([2.0 ** (-8.0 * (i + 1) / DIFF_HEADS) for i in range(DIFF_HEADS)], F32)

    h = x.reshape(T, D)
    hb = h.astype(BF16)
    for l in range(depth):
        lambda_init = 0.8 - 0.6 * math.exp(-0.3 * l)
        lam = jnp.exp(jnp.sum(lam_q1[l] * lam_k1[l])) - jnp.exp(jnp.sum(lam_q2[l] * lam_k2[l])) + lambda_init
        scalars = jnp.stack([lam, jnp.asarray(1.0 - lambda_init, F32)]).astype(F32)

        proj = _in_proj(hb, w_in[l].astype(BF16)).reshape(B, S, -1)
        diff = _diff_attn(proj, scalars, slopes, subln_g[l].reshape(1, SLOT))
        sb = _sb_attn(proj)
        h, hb = _out_proj_ln(h, diff.reshape(T, ATT_WIDTH), sb.reshape(T, ATT_WIDTH), w_o[l].astype(BF16),
                             ln1_g[l].reshape(1, D), ln1_b[l].reshape(1, D), alpha=alpha)

        keys = sub_keys[l].reshape(2 * PEER_HEADS, N_KEYS, HALF_Q).astype(BF16)
        tables = _peer_route(hb, w_query[l].T.astype(BF16), keys)
        h, hb = _peer_expert(h, hb, expert_u[l].astype(BF16), expert_v[l].T.astype(BF16), tables,
                             ln2_g[l].reshape(1, D), ln2_b[l].reshape(1, D), alpha=alpha)
    return h.reshape(B, S, D)
```

```python
import functools
import math

import jax
import jax.numpy as jnp
from jax import lax
from jax.experimental import pallas as pl
from jax.experimental.pallas import tpu as pltpu

F32 = jnp.float32
BF16 = jnp.bfloat16

HEAD_DIM = 64
DIFF_HEADS = 4
SB_HEADS = 8
SLOT = 2 * HEAD_DIM
DIFF_SLOTS = DIFF_HEADS
SB_SLOTS = SB_HEADS // 2
ATT_WIDTH = DIFF_SLOTS * SLOT
QUERY_GROUPS = (0, 3)
LOG2E = math.log2(math.e)
QUERY_SCALE = HEAD_DIM ** -0.5 * LOG2E

PEER_HEADS = 8
N_KEYS = 128
PEER_TOPK = 16
HALF_Q = 128

LN_EPS = 1e-5
RMS_EPS = 1e-5

VMEM_LIMIT = 56 * 1024 * 1024
NEG_INF = float("-inf")


def _params(sem):
    return pltpu.CompilerParams(dimension_semantics=sem, vmem_limit_bytes=VMEM_LIMIT)


def _in_proj_kernel(x_ref, w_ref, o_ref, *, tn):
    x = x_ref[...]
    for n0 in range(0, o_ref.shape[1], tn):
        y = jnp.dot(x, w_ref[:, n0:n0 + tn], preferred_element_type=F32)
        if n0 // ATT_WIDTH in QUERY_GROUPS:
            y = y * QUERY_SCALE
        o_ref[:, n0:n0 + tn] = y.astype(o_ref.dtype)


def _in_proj(xb, wb, *, tm=512, tn=ATT_WIDTH):
    T, K = xb.shape
    N = wb.shape[1]
    return pl.pallas_call(
        functools.partial(_in_proj_kernel, tn=tn),
        grid=(T // tm,),
        in_specs=[pl.BlockSpec((tm, K), lambda i: (i, 0)),
                  pl.BlockSpec((K, N), lambda i: (0, 0))],
        out_specs=pl.BlockSpec((tm, N), lambda i: (i, 0)),
        out_shape=jax.ShapeDtypeStruct((T, N), BF16),
        compiler_params=_params(("parallel",)),
        name="in_proj",
    )(xb, wb)


def _nt_dot(a, b):
    return lax.dot_general(a, b, (((1,), (1,)), ((), ())), preferred_element_type=F32)


def _diff_attn_kernel(sc_ref, slope_ref, q_ref, k_ref, v_ref, g_ref, o_ref, *, blk):
    hd = pl.program_id(1)
    qi = pl.program_id(2)
    lam = sc_ref[0]
    post = sc_ref[1]
    slope = slope_ref[hd]

    lane = lax.broadcasted_iota(jnp.int32, (blk, SLOT), 1)
    q = q_ref[...]
    zero = jnp.zeros_like(q)
    qs = jnp.concatenate([jnp.where(lane < HEAD_DIM, q, zero), jnp.where(lane >= HEAD_DIM, q, zero)], axis=0)

    colf = lax.broadcasted_iota(jnp.int32, (1, blk), 1).astype(F32)
    ones = jnp.ones((blk, SLOT), BF16)

    def step(j, carry, masked):
        m_old, a_old = carry
        kb = k_ref[pl.ds(pl.multiple_of(j * blk, blk), blk), :]
        vb = v_ref[pl.ds(pl.multiple_of(j * blk, blk), blk), :]
        vaug = jnp.concatenate([vb, ones], axis=1)
        bias = (slope * LOG2E) * (colf + ((j - qi) * blk).astype(F32))
        s = _nt_dot(qs, kb) + bias
        if masked:
            t = lax.broadcasted_iota(jnp.int32, (2 * blk, blk), 0) & (blk - 1)
            s = jnp.where(lax.broadcasted_iota(jnp.int32, (2 * blk, blk), 1) <= t, s, NEG_INF)
        m_new = jnp.maximum(m_old, jnp.max(s, axis=1, keepdims=True))
        alpha = jnp.exp2(m_old - m_new)
        p = jnp.exp2(s - m_new).astype(BF16)
        return m_new, alpha * a_old + jnp.dot(p, vaug, preferred_element_type=F32)

    init = (jnp.full((2 * blk, 1), NEG_INF, F32), jnp.zeros((2 * blk, 2 * SLOT), F32))
    carry = lax.fori_loop(0, qi, lambda j, c: step(j, c, False), init)
    _, a = step(qi, carry, True)
    a1, a2 = a[:blk], a[blk:]

    d = a1[:, :SLOT] / a1[:, SLOT:] - lam * (a2[:, :SLOT] / a2[:, SLOT:])
    d = d * lax.rsqrt(jnp.mean(d * d, axis=1, keepdims=True) + RMS_EPS)
    o_ref[...] = (d * g_ref[...] * post).astype(o_ref.dtype)


def _diff_attn(proj, scalars, slopes, subln_g, *, blk=512):
    B, S, _ = proj.shape
    kblk = ATT_WIDTH // SLOT
    return pl.pallas_call(
        functools.partial(_diff_attn_kernel, blk=blk),
        grid=(B, DIFF_SLOTS, S // blk),
        in_specs=[pl.BlockSpec(memory_space=pltpu.SMEM),
                  pl.BlockSpec(memory_space=pltpu.SMEM),
                  pl.BlockSpec((None, blk, SLOT), lambda b, h, i: (b, i, h)),
                  pl.BlockSpec((None, S, SLOT), lambda b, h, i: (b, 0, kblk + h)),
                  pl.BlockSpec((None, S, SLOT), lambda b, h, i: (b, 0, 2 * kblk + h)),
                  pl.BlockSpec((1, SLOT), lambda b, h, i: (0, 0))],
        out_specs=pl.BlockSpec((None, blk, SLOT), lambda b, h, i: (b, i, h)),
        out_shape=jax.ShapeDtypeStruct((B, S, ATT_WIDTH), BF16),
        compiler_params=_params(("parallel", "parallel", "parallel")),
        name="diff_attn",
    )(scalars, slopes, proj, proj, proj, subln_g)


def _sb_attn_kernel(q_ref, k_ref, v_ref, o_ref, *, tq, tk, n_part):
    qi = pl.program_id(2)
    nsub = tq // tk
    lane = lax.broadcasted_iota(jnp.int32, (tq, SLOT), 1)
    q = q_ref[...]
    zero = jnp.zeros_like(q)
    qs = jnp.concatenate([jnp.where(lane < HEAD_DIM, q, zero), jnp.where(lane >= HEAD_DIM, q, zero)], axis=0)

    row = lax.broadcasted_iota(jnp.int32, (tk, tk), 0)
    col = lax.broadcasted_iota(jnp.int32, (tk, tk), 1)
    half = jnp.concatenate([jnp.where(row > col, -1.0, 0.0).astype(BF16), jnp.full((tk, tk), -1.0, BF16)], axis=1)
    cum_w = jnp.concatenate([half, half], axis=0)
    sign_bit = jnp.int32(-2 ** 31)

    sub = [slice(s * tk, (s + 1) * tk) for s in range(nsub)]
    rp = 2 * tq // n_part
    parts = [slice(p * rp, (p + 1) * rp) for p in range(n_part)]

    def chunk(c, carry, masked):
        later, acc = carry
        off = pl.multiple_of(c * tq, tq)
        kb = k_ref[pl.ds(off, tq), :]
        vb = v_ref[pl.ds(off, tq), :]
        zs = [_nt_dot(qs[pr], kb) for pr in parts]
        mid = []
        for p, z in enumerate(zs):
            neg_abs = lax.bitcast_convert_type(lax.bitcast_convert_type(z, jnp.int32) | sign_bit, F32)
            sp = jnp.maximum(z, 0.0) + LOG2E * jnp.log(1.0 + jnp.exp2(neg_abs))
            ls = z - sp
            lf = sp
            strict = None
            if masked:
                t = (lax.broadcasted_iota(jnp.int32, z.shape, 0) + p * rp) & (tq - 1)
                strict = lax.broadcasted_iota(jnp.int32, z.shape, 1) < t
                lf = jnp.where(strict, lf, 0.0)
            hi = lf.astype(BF16)
            lo = (lf - hi.astype(F32)).astype(BF16)
            stacked = jnp.concatenate([jnp.concatenate([hi[:, c_], lo[:, c_]], axis=1) for c_ in sub], axis=0)
            cs = jnp.dot(stacked, cum_w, preferred_element_type=F32)
            mid.append((ls, cs, strict))
        later_out, acc_out = [], []
        for pr, (ls, cs, strict) in zip(parts, mid):
            lat = later[pr]
            ws = [None] * nsub
            for s in reversed(range(nsub)):
                cs_s = cs[s * rp:(s + 1) * rp]
                w = jnp.exp2(ls[:, sub[s]] + cs_s[:, :tk] + lat)
                if masked:
                    w = jnp.where(strict[:, sub[s]], w, 0.0)
                ws[s] = w.astype(BF16)
                lat = lat + cs_s[:, tk:]
            later_out.append(lat)
            acc_out.append(acc[pr] + jnp.dot(jnp.concatenate(ws, axis=1), vb, preferred_element_type=F32))
        return jnp.concatenate(later_out, axis=0), jnp.concatenate(acc_out, axis=0)

    init = (jnp.zeros((2 * tq, SLOT), F32), jnp.zeros((2 * tq, SLOT), F32))
    carry = chunk(qi, init, True)
    _, acc = lax.fori_loop(0, qi, lambda jj, c: chunk(qi - 1 - jj, c, False), carry)
    o_ref[...] = jnp.where(lane < HEAD_DIM, acc[:tq], acc[tq:]).astype(o_ref.dtype)


def _sb_attn(proj, *, tq=512, tk=128, n_part=2):
    B, S, _ = proj.shape
    kblk = ATT_WIDTH // SLOT
    base = 3 * kblk
    return pl.pallas_call(
        functools.partial(_sb_attn_kernel, tq=tq, tk=tk, n_part=n_part),
        grid=(B, SB_SLOTS, S // tq),
        in_specs=[pl.BlockSpec((None, tq, SLOT), lambda b, p, i: (b, i, base + p)),
                  pl.BlockSpec((None, S, SLOT), lambda b, p, i: (b, 0, base + kblk + p)),
                  pl.BlockSpec((None, S, SLOT), lambda b, p, i: (b, 0, base + 2 * kblk + p))],
        out_specs=pl.BlockSpec((None, tq, SLOT), lambda b, p, i: (b, i, p)),
        out_shape=jax.ShapeDtypeStruct((B, S, ATT_WIDTH), BF16),
        compiler_params=_params(("parallel", "parallel", "parallel")),
        name="sb_attn",
    )(proj, proj, proj)


def _layer_norm(r, g, b):
    mu = jnp.mean(r, axis=-1, keepdims=True)
    c = r - mu
    var = jnp.mean(c * c, axis=-1, keepdims=True)
    return c * lax.rsqrt(var + LN_EPS) * g + b


def _out_proj_ln_kernel(h_ref, a_ref, b_ref, w_ref, g_ref, beta_ref, o_ref, ob_ref, *, alpha):
    half = a_ref.shape[1]
    y = jnp.dot(a_ref[...], w_ref[:half, :], preferred_element_type=F32)
    y = y + jnp.dot(b_ref[...], w_ref[half:, :], preferred_element_type=F32)
    out = _layer_norm(alpha * h_ref[...] + y, g_ref[...], beta_ref[...])
    o_ref[...] = out
    ob_ref[...] = out.astype(BF16)


def _out_proj_ln(h, a, b, wb, g, beta, *, alpha, tm=512):
    T, D = h.shape
    half = a.shape[1]
    return pl.pallas_call(
        functools.partial(_out_proj_ln_kernel, alpha=alpha),
        grid=(T // tm,),
        in_specs=[pl.BlockSpec((tm, D), lambda i: (i, 0)),
                  pl.BlockSpec((tm, half), lambda i: (i, 0)),
                  pl.BlockSpec((tm, half), lambda i: (i, 0)),
                  pl.BlockSpec((2 * half, D), lambda i: (0, 0)),
                  pl.BlockSpec((1, D), lambda i: (0, 0)),
                  pl.BlockSpec((1, D), lambda i: (0, 0))],
        out_specs=[pl.BlockSpec((tm, D), lambda i: (i, 0)),
                   pl.BlockSpec((tm, D), lambda i: (i, 0))],
        out_shape=[jax.ShapeDtypeStruct((T, D), F32), jax.ShapeDtypeStruct((T, D), BF16)],
        compiler_params=_params(("parallel",)),
        name="out_proj_ln",
    )(h, a, b, wb, g, beta)


def _merge_sort_network(lo, hi):
    def merge(lo, hi, r):
        step = 2 * r
        if step < hi - lo:
            yield from merge(lo, hi, step)
            yield from merge(lo + r, hi, step)
            yield from ((i, i + r) for i in range(lo + r, hi - r, step))
        else:
            yield (lo, lo + r)
    if hi - lo >= 1:
        mid = lo + (hi - lo) // 2
        yield from _merge_sort_network(lo, mid)
        yield from _merge_sort_network(mid + 1, hi)
        yield from merge(lo, hi, 1)


def _top16(s):
    n = s.shape[0] // 8
    v = [s[8 * r:8 * (r + 1), :] for r in range(n)]
    for i, j in _merge_sort_network(0, n - 1):
        v[i], v[j] = jnp.maximum(v[i], v[j]), jnp.minimum(v[i], v[j])
    vals = []
    for k in range(PEER_TOPK):
        m = jnp.max(v[0], axis=0, keepdims=True)
        vals.append(m)
        eq = v[0] == m
        live = PEER_TOPK - 1 - k
        for r in range(live):
            v[r] = jnp.where(eq, v[r + 1], v[r])
    return jnp.concatenate(vals, axis=0)


def _rank_among(s, v):
    rank = jnp.zeros(s.shape, F32)
    for k in range(PEER_TOPK):
        rank = rank + jnp.where(s < v[k:k + 1, :], 1.0, 0.0)
    return rank


def _peer_route_kernel(x_ref, wq_ref, keys_ref, rank2_ref, b_ref, cnt_ref, a_ref):
    tb = x_ref.shape[0]
    qt = _nt_dot(wq_ref[...], x_ref[...]).astype(BF16)
    row8 = lax.broadcasted_iota(jnp.int32, (8, tb), 0)
    row16 = lax.broadcasted_iota(jnp.int32, (PEER_TOPK, tb), 0)
    for h in range(PEER_HEADS):
        s1 = jnp.dot(keys_ref[2 * h], qt[(2 * h) * HALF_Q:(2 * h + 1) * HALF_Q, :], preferred_element_type=F32)
        s2 = jnp.dot(keys_ref[2 * h + 1], qt[(2 * h + 1) * HALF_Q:(2 * h + 2) * HALF_Q, :],
                     preferred_element_type=F32)
        v1 = _top16(s1)
        v2 = _top16(s2)
        rank2 = _rank_among(s2, v2)
        v2lo = v2[0:8, :]
        pieces = [v1 + v2[0:1, :],
                  v1[0:8, :] + v2[1:2, :],
                  jnp.where(row16 >= 2, v1[0:1, :] + v2, NEG_INF),
                  jnp.where(row8 >= 2, v1[1:2, :] + v2lo, NEG_INF),
                  jnp.where((row8 >= 2) & (row8 <= 4), v1[2:3, :] + v2lo, NEG_INF),
                  jnp.where((row8 >= 2) & (row8 <= 3), v1[3:4, :] + v2lo, NEG_INF),
                  jnp.where(row8 == 2, v1[4:5, :] + v2lo, NEG_INF)]
        cand = jnp.concatenate(pieces, axis=0)
        work = cand
        tau = None
        for _ in range(PEER_TOPK):
            tau = jnp.max(work, axis=0, keepdims=True)
            work = jnp.where(work == tau, NEG_INF, work)
        sel = cand >= tau
        top = v1[0:1, :] + v2[0:1, :]
        z = jnp.sum(jnp.where(sel, jnp.exp(cand - top), 0.0), axis=0, keepdims=True)
        self32 = jnp.where(sel, 1.0, 0.0)
        per_row = self32[0:16] + jnp.concatenate([self32[16:24], jnp.zeros((8, tb), F32)], axis=0)
        off = 24
        for k1, rows in enumerate((16, 8, 8, 8, 8)):
            per_row = per_row + jnp.where(row16 == k1, jnp.sum(self32[off:off + rows], axis=0, keepdims=True), 0.0)
            off += rows
        cnt = jnp.zeros(s1.shape, F32)
        for k1 in range(PEER_TOPK):
            cnt = jnp.where(s1 == v1[k1:k1 + 1, :], per_row[k1:k1 + 1, :], cnt)
        rank2_ref[h] = rank2.astype(BF16)
        b_ref[h] = jnp.exp(s2 - v2[0:1, :]).astype(BF16)
        cnt_ref[h] = cnt
        a_ref[h] = jnp.exp(s1 - v1[0:1, :]) * (0.5 / z)


def _peer_route(xb, wqt, keys, *, tb=256):
    T, D = xb.shape
    tab_spec = pl.BlockSpec((PEER_HEADS, N_KEYS, tb), lambda i: (0, 0, i))
    return pl.pallas_call(
        _peer_route_kernel,
        grid=(T // tb,),
        in_specs=[pl.BlockSpec((tb, D), lambda i: (i, 0)),
                  pl.BlockSpec(wqt.shape, lambda i: (0, 0)),
                  pl.BlockSpec(keys.shape, lambda i: (0, 0, 0))],
        out_specs=[tab_spec] * 4,
        out_shape=[jax.ShapeDtypeStruct((PEER_HEADS, N_KEYS, T), dt) for dt in (BF16, BF16, F32, F32)],
        compiler_params=_params(("parallel",)),
        name="peer_route",
    )(xb, wqt, keys)


def _peer_expert_kernel(h_ref, x_ref, u_ref, vt_ref, rank2_ref, b_ref, cnt_ref, a_ref, g_ref, beta_ref,
                        o_ref, ob_ref, acc_ref, hid_ref, *, alpha):
    e = pl.program_id(1)

    @pl.when(e == 0)
    def _():
        acc_ref[...] = jnp.zeros_like(acc_ref)

    tb = x_ref.shape[0]
    pack = 16
    zero = jnp.zeros((N_KEYS // pack, pack, tb), BF16)

    def row(ref, h, ii):
        return jnp.broadcast_to(ref[h, ii:ii + 1, :], (pack, tb)).astype(BF16)[None]

    act = _nt_dot(u_ref[...], x_ref[...])
    n_i = u_ref.shape[0] // N_KEYS
    per = n_i // 2
    csum = None
    for ii in range(n_i):
        gate = zero
        for h in range(PEER_HEADS):
            sel = rank2_ref[h].reshape(zero.shape) < row(cnt_ref, h, ii)
            gate = gate + jnp.where(sel, b_ref[h].reshape(zero.shape), zero) * row(a_ref, h, ii)
        a = act[ii * N_KEYS:(ii + 1) * N_KEYS, :]
        hid = gate.reshape(N_KEYS, tb) * (a * (1.0 + lax.erf(a * math.sqrt(0.5)))).astype(BF16)
        hid_ref[ii * N_KEYS:(ii + 1) * N_KEYS, :] = hid
        if (ii + 1) % per == 0:
            rows = slice((ii + 1 - per) * N_KEYS, (ii + 1) * N_KEYS)
            part = jnp.dot(vt_ref[:, rows], hid_ref[rows, :], preferred_element_type=F32)
            csum = part if csum is None else csum + part
    acc_ref[...] += csum

    @pl.when(e == pl.num_programs(1) - 1)
    def _():
        y = acc_ref[...].T
        out = _layer_norm(alpha * h_ref[...] + y, g_ref[...], beta_ref[...])
        o_ref[...] = out
        ob_ref[...] = out.astype(BF16)


def _peer_expert(h, xb, ub, vtb, tables, g, beta, *, alpha, tb=512, eb=2048):
    T, D = h.shape
    E = ub.shape[0]
    rank2, bexp, cnt, a = tables
    n_i = eb // N_KEYS
    full_tab = pl.BlockSpec((PEER_HEADS, N_KEYS, tb), lambda t, e: (0, 0, t))
    row_tab = pl.BlockSpec((PEER_HEADS, n_i, tb), lambda t, e: (0, e, t))
    return pl.pallas_call(
        functools.partial(_peer_expert_kernel, alpha=alpha),
        grid=(T // tb, E // eb),
        in_specs=[pl.BlockSpec((tb, D), lambda t, e: (t, 0)),
                  pl.BlockSpec((tb, D), lambda t, e: (t, 0)),
                  pl.BlockSpec((eb, D), lambda t, e: (e, 0)),
                  pl.BlockSpec((D, eb), lambda t, e: (0, e)),
                  full_tab, full_tab, row_tab, row_tab,
                  pl.BlockSpec((1, D), lambda t, e: (0, 0)),
                  pl.BlockSpec((1, D), lambda t, e: (0, 0))],
        out_specs=[pl.BlockSpec((tb, D), lambda t, e: (t, 0)),
                   pl.BlockSpec((tb, D), lambda t, e: (t, 0))],
        out_shape=[jax.ShapeDtypeStruct((T, D), F32), jax.ShapeDtypeStruct((T, D), BF16)],
        scratch_shapes=[pltpu.VMEM((D, tb), F32), pltpu.VMEM((eb, tb), BF16)],
        compiler_params=_params(("parallel", "arbitrary")),
        name="peer_expert",
    )(h, xb, ub, vtb, rank2, bexp, cnt, a, g, beta)


def kernel(x, w_in, lam_q1, lam_k1, lam_q2, lam_k2, subln_g, w_o, ln1_g, ln1_b, w_query, sub_keys, expert_u,
           expert_v, ln2_g, ln2_b):
    B, S, D = x.shape
    T = B * S
    depth = w_in.shape[0]
    alpha = (2.0 * depth) ** 0.25
    slopes = jnp.asarray([2.0 ** (-8.0 * (i + 1) / DIFF_HEADS) for i in range(DIFF_HEADS)], F32)

    h = x.reshape(T, D)
    hb = h.astype(BF16)
    for l in range(depth):
        lambda_init = 0.8 - 0.6 * math.exp(-0.3 * l)
        lam = jnp.exp(jnp.sum(lam_q1[l] * lam_k1[l])) - jnp.exp(jnp.sum(lam_q2[l] * lam_k2[l])) + lambda_init
        scalars = jnp.stack([lam, jnp.asarray(1.0 - lambda_init, F32)]).astype(F32)

        proj = _in_proj(hb, w_in[l].astype(BF16)).reshape(B, S, -1)
        diff = _diff_attn(proj, scalars, slopes, subln_g[l].reshape(1, SLOT))
        sb = _sb_attn(proj)
        h, hb = _out_proj_ln(h, diff.reshape(T, ATT_WIDTH), sb.reshape(T, ATT_WIDTH), w_o[l].astype(BF16),
                             ln1_g[l].reshape(1, D), ln1_b[l].reshape(1, D), alpha=alpha)

        keys = sub_keys[l].reshape(2 * PEER_HEADS, N_KEYS, HALF_Q).astype(BF16)
        tables = _peer_route(hb, w_query[l].T.astype(BF16), keys)
        h, hb = _peer_expert(h, hb, expert_u[l].astype(BF16), expert_v[l].T.astype(BF16), tables,
                             ln2_g[l].reshape(1, D), ln2_b[l].reshape(1, D), alpha=alpha)
    return h.reshape(B, S, D)
```

```python
import functools
import math

import jax
import jax.numpy as jnp
from jax import lax
from jax.experimental import pallas as pl
from jax.experimental.pallas import tpu as pltpu

F32 = jnp.float32
BF16 = jnp.bfloat16

HEAD_DIM = 64
DIFF_HEADS = 4
SB_HEADS = 8
SLOT = 2 * HEAD_DIM
DIFF_SLOTS = DIFF_HEADS
SB_SLOTS = SB_HEADS // 2
ATT_WIDTH = DIFF_SLOTS * SLOT
QUERY_GROUPS = (0, 3)
LOG2E = math.log2(math.e)
QUERY_SCALE = HEAD_DIM ** -0.5 * LOG2E
SQRT_HALF = math.sqrt(0.5)

PEER_HEADS = 8
N_KEYS = 128
PEER_TOPK = 16
HALF_Q = 128

LN_EPS = 1e-5
RMS_EPS = 1e-5

VMEM_LIMIT = 56 * 1024 * 1024
NEG_INF = float("-inf")


def _params(sem):
    return pltpu.CompilerParams(dimension_semantics=sem, vmem_limit_bytes=VMEM_LIMIT)


def _in_proj_kernel(x_ref, w_ref, o_ref, *, tn):
    x = x_ref[...]
    for n0 in range(0, o_ref.shape[1], tn):
        y = jnp.dot(x, w_ref[:, n0:n0 + tn], preferred_element_type=F32)
        if n0 // ATT_WIDTH in QUERY_GROUPS:
            y = y * QUERY_SCALE
        o_ref[:, n0:n0 + tn] = y.astype(o_ref.dtype)


def _in_proj(xb, wb, *, tm=512, tn=ATT_WIDTH):
    T, K = xb.shape
    N = wb.shape[1]
    return pl.pallas_call(
        functools.partial(_in_proj_kernel, tn=tn),
        grid=(T // tm,),
        in_specs=[pl.BlockSpec((tm, K), lambda i: (i, 0)),
                  pl.BlockSpec((K, N), lambda i: (0, 0))],
        out_specs=pl.BlockSpec((tm, N), lambda i: (i, 0)),
        out_shape=jax.ShapeDtypeStruct((T, N), BF16),
        compiler_params=_params(("parallel",)),
        name="in_proj",
    )(xb, wb)


def _nt_dot(a, b):
    return lax.dot_general(a, b, (((1,), (1,)), ((), ())), preferred_element_type=F32)


def _diff_attn_kernel(sc_ref, slope_ref, q_ref, k_ref, v_ref, g_ref, o_ref, *, blk):
    hd = pl.program_id(1)
    qi = pl.program_id(2)
    lam = sc_ref[0]
    post = sc_ref[1]
    slope = slope_ref[hd]

    lane = lax.broadcasted_iota(jnp.int32, (blk, SLOT), 1)
    q = q_ref[...]
    zero = jnp.zeros_like(q)
    qs = jnp.concatenate([jnp.where(lane < HEAD_DIM, q, zero), jnp.where(lane >= HEAD_DIM, q, zero)], axis=0)

    colf = lax.broadcasted_iota(jnp.int32, (1, blk), 1).astype(F32)
    ones = jnp.ones((blk, SLOT), BF16)

    def step(j, carry, masked):
        m_old, a_old = carry
        kb = k_ref[pl.ds(pl.multiple_of(j * blk, blk), blk), :]
        vb = v_ref[pl.ds(pl.multiple_of(j * blk, blk), blk), :]
        vaug = jnp.concatenate([vb, ones], axis=1)
        bias = (slope * LOG2E) * (colf + ((j - qi) * blk).astype(F32))
        s = _nt_dot(qs, kb) + bias
        if masked:
            t = lax.broadcasted_iota(jnp.int32, (2 * blk, blk), 0) & (blk - 1)
            s = jnp.where(lax.broadcasted_iota(jnp.int32, (2 * blk, blk), 1) <= t, s, NEG_INF)
        m_new = jnp.maximum(m_old, jnp.max(s, axis=1, keepdims=True))
        alpha = jnp.exp2(m_old - m_new)
        p = jnp.exp2(s - m_new).astype(BF16)
        return m_new, alpha * a_old + jnp.dot(p, vaug, preferred_element_type=F32)

    init = (jnp.full((2 * blk, 1), NEG_INF, F32), jnp.zeros((2 * blk, 2 * SLOT), F32))
    carry = lax.fori_loop(0, qi, lambda j, c: step(j, c, False), init)
    _, a = step(qi, carry, True)
    a1, a2 = a[:blk], a[blk:]

    d = a1[:, :SLOT] / a1[:, SLOT:] - lam * (a2[:, :SLOT] / a2[:, SLOT:])
    d = d * lax.rsqrt(jnp.mean(d * d, axis=1, keepdims=True) + RMS_EPS)
    o_ref[...] = (d * g_ref[...] * post).astype(o_ref.dtype)


def _diff_attn(proj, scalars, slopes, subln_g, *, blk=512):
    B, S, _ = proj.shape
    kblk = ATT_WIDTH // SLOT
    return pl.pallas_call(
        functools.partial(_diff_attn_kernel, blk=blk),
        grid=(B, DIFF_SLOTS, S // blk),
        in_specs=[pl.BlockSpec(memory_space=pltpu.SMEM),
                  pl.BlockSpec(memory_space=pltpu.SMEM),
                  pl.BlockSpec((None, blk, SLOT), lambda b, h, i: (b, i, h)),
                  pl.BlockSpec((None, S, SLOT), lambda b, h, i: (b, 0, kblk + h)),
                  pl.BlockSpec((None, S, SLOT), lambda b, h, i: (b, 0, 2 * kblk + h)),
                  pl.BlockSpec((1, SLOT), lambda b, h, i: (0, 0))],
        out_specs=pl.BlockSpec((None, blk, SLOT), lambda b, h, i: (b, i, h)),
        out_shape=jax.ShapeDtypeStruct((B, S, ATT_WIDTH), BF16),
        compiler_params=_params(("parallel", "parallel", "parallel")),
        name="diff_attn",
    )(scalars, slopes, proj, proj, proj, subln_g)


def _sb_attn_kernel(q_ref, k_ref, v_ref, o_ref, *, tq, tk, n_part):
    qi = pl.program_id(2)
    nsub = tq // tk
    lane = lax.broadcasted_iota(jnp.int32, (tq, SLOT), 1)
    q = q_ref[...]
    zero = jnp.zeros_like(q)
    qs = jnp.concatenate([jnp.where(lane < HEAD_DIM, q, zero), jnp.where(lane >= HEAD_DIM, q, zero)], axis=0)

    row = lax.broadcasted_iota(jnp.int32, (tk, tk), 0)
    col = lax.broadcasted_iota(jnp.int32, (tk, tk), 1)
    half = jnp.concatenate([jnp.where(row > col, -1.0, 0.0).astype(BF16), jnp.full((tk, tk), -1.0, BF16)], axis=1)
    cum_w = jnp.concatenate([half, half], axis=0)
    sign_bit = jnp.int32(-2 ** 31)

    sub = [slice(s * tk, (s + 1) * tk) for s in range(nsub)]
    rp = 2 * tq // n_part
    parts = [slice(p * rp, (p + 1) * rp) for p in range(n_part)]

    def chunk(c, carry, masked):
        later, acc = carry
        off = pl.multiple_of(c * tq, tq)
        kb = k_ref[pl.ds(off, tq), :]
        vb = v_ref[pl.ds(off, tq), :]
        zs = [_nt_dot(qs[pr], kb) for pr in parts]
        mid = []
        for p, z in enumerate(zs):
            neg_abs = lax.bitcast_convert_type(lax.bitcast_convert_type(z, jnp.int32) | sign_bit, F32)
            sp = jnp.maximum(z, 0.0) + LOG2E * jnp.log(1.0 + jnp.exp2(neg_abs))
            ls = z - sp
            lf = sp
            strict = None
            if masked:
                t = (lax.broadcasted_iota(jnp.int32, z.shape, 0) + p * rp) & (tq - 1)
                strict = lax.broadcasted_iota(jnp.int32, z.shape, 1) < t
                lf = jnp.where(strict, lf, 0.0)
            hi = lf.astype(BF16)
            lo = (lf - hi.astype(F32)).astype(BF16)
            stacked = jnp.concatenate([jnp.concatenate([hi[:, c_], lo[:, c_]], axis=1) for c_ in sub], axis=0)
            cs = jnp.dot(stacked, cum_w, preferred_element_type=F32)
            mid.append((ls, cs, strict))
        later_out, acc_out = [], []
        for pr, (ls, cs, strict) in zip(parts, mid):
            lat = later[pr]
            ws = [None] * nsub
            for s in reversed(range(nsub)):
                cs_s = cs[s * rp:(s + 1) * rp]
                w = jnp.exp2(ls[:, sub[s]] + cs_s[:, :tk] + lat)
                if masked:
                    w = jnp.where(strict[:, sub[s]], w, 0.0)
                ws[s] = w.astype(BF16)
                lat = lat + cs_s[:, tk:]
            later_out.append(lat)
            acc_out.append(acc[pr] + jnp.dot(jnp.concatenate(ws, axis=1), vb, preferred_element_type=F32))
        return jnp.concatenate(later_out, axis=0), jnp.concatenate(acc_out, axis=0)

    init = (jnp.zeros((2 * tq, SLOT), F32), jnp.zeros((2 * tq, SLOT), F32))
    carry = chunk(qi, init, True)
    _, acc = lax.fori_loop(0, qi, lambda jj, c: chunk(qi - 1 - jj, c, False), carry)
    o_ref[...] = jnp.where(lane < HEAD_DIM, acc[:tq], acc[tq:]).astype(o_ref.dtype)


def _sb_attn(proj, *, tq=512, tk=128, n_part=2):
    B, S, _ = proj.shape
    kblk = ATT_WIDTH // SLOT
    base = 3 * kblk
    return pl.pallas_call(
        functools.partial(_sb_attn_kernel, tq=tq, tk=tk, n_part=n_part),
        grid=(B, SB_SLOTS, S // tq),
        in_specs=[pl.BlockSpec((None, tq, SLOT), lambda b, p, i: (b, i, base + p)),
                  pl.BlockSpec((None, S, SLOT), lambda b, p, i: (b, 0, base + kblk + p)),
                  pl.BlockSpec((None, S, SLOT), lambda b, p, i: (b, 0, base + 2 * kblk + p))],
        out_specs=pl.BlockSpec((None, tq, SLOT), lambda b, p, i: (b, i, p)),
        out_shape=jax.ShapeDtypeStruct((B, S, ATT_WIDTH), BF16),
        compiler_params=_params(("parallel", "parallel", "parallel")),
        name="sb_attn",
    )(proj, proj, proj)


def _layer_norm(r, g, b):
    mu = jnp.mean(r, axis=-1, keepdims=True)
    c = r - mu
    var = jnp.mean(c * c, axis=-1, keepdims=True)
    return c * lax.rsqrt(var + LN_EPS) * g + b


def _out_proj_ln_kernel(h_ref, a_ref, b_ref, w_ref, g_ref, beta_ref, o_ref, ob_ref, *, alpha):
    half = a_ref.shape[1]
    y = jnp.dot(a_ref[...], w_ref[:half, :], preferred_element_type=F32)
    y = y + jnp.dot(b_ref[...], w_ref[half:, :], preferred_element_type=F32)
    out = _layer_norm(alpha * h_ref[...] + y, g_ref[...], beta_ref[...])
    o_ref[...] = out
    ob_ref[...] = out.astype(BF16)


def _out_proj_ln(h, a, b, wb, g, beta, *, alpha, tm=512):
    T, D = h.shape
    half = a.shape[1]
    return pl.pallas_call(
        functools.partial(_out_proj_ln_kernel, alpha=alpha),
        grid=(T // tm,),
        in_specs=[pl.BlockSpec((tm, D), lambda i: (i, 0)),
                  pl.BlockSpec((tm, half), lambda i: (i, 0)),
                  pl.BlockSpec((tm, half), lambda i: (i, 0)),
                  pl.BlockSpec((2 * half, D), lambda i: (0, 0)),
                  pl.BlockSpec((1, D), lambda i: (0, 0)),
                  pl.BlockSpec((1, D), lambda i: (0, 0))],
        out_specs=[pl.BlockSpec((tm, D), lambda i: (i, 0)),
                   pl.BlockSpec((tm, D), lambda i: (i, 0))],
        out_shape=[jax.ShapeDtypeStruct((T, D), F32), jax.ShapeDtypeStruct((T, D), BF16)],
        compiler_params=_params(("parallel",)),
        name="out_proj_ln",
    )(h, a, b, wb, g, beta)


def _merge_sort_network(lo, hi):
    def merge(lo, hi, r):
        step = 2 * r
        if step < hi - lo:
            yield from merge(lo, hi, step)
            yield from merge(lo + r, hi, step)
            yield from ((i, i + r) for i in range(lo + r, hi - r, step))
        else:
            yield (lo, lo + r)
    if hi - lo >= 1:
        mid = lo + (hi - lo) // 2
        yield from _merge_sort_network(lo, mid)
        yield from _merge_sort_network(mid + 1, hi)
        yield from merge(lo, hi, 1)


def _sort_network(n):
    if n & (n - 1) == 0:
        return list(_merge_sort_network(0, n - 1))
    return [(i, i + 1) for rnd in range(n) for i in range(rnd % 2, n - 1, 2)]


def _top16(s):
    n = s.shape[0] // 8
    v = [s[8 * r:8 * (r + 1), :] for r in range(n)]
    for i, j in _sort_network(n):
        v[i], v[j] = jnp.maximum(v[i], v[j]), jnp.minimum(v[i], v[j])
    vals = []
    for k in range(PEER_TOPK):
        m = jnp.max(v[0], axis=0, keepdims=True)
        vals.append(m)
        eq = v[0] == m
        for r in range(min(n, PEER_TOPK - 1 - k)):
            v[r] = jnp.where(eq, v[r + 1] if r + 1 < n else NEG_INF, v[r])
    return jnp.concatenate(vals, axis=0)


def _rank_among(s, v):
    rank = jnp.full(s.shape, float(PEER_TOPK), F32)
    for k in reversed(range(PEER_TOPK)):
        rank = jnp.where(s == v[k:k + 1, :], float(k), rank)
    return rank


def _peer_route_kernel(x_ref, wq_ref, keys_ref, rank2_ref, b_ref, cnt_ref, a_ref):
    tb = x_ref.shape[0]
    qt = _nt_dot(wq_ref[...], x_ref[...]).astype(BF16)
    row8 = lax.broadcasted_iota(jnp.int32, (8, tb), 0)
    row16 = lax.broadcasted_iota(jnp.int32, (PEER_TOPK, tb), 0)
    for h in range(PEER_HEADS):
        s1 = jnp.dot(keys_ref[2 * h], qt[(2 * h) * HALF_Q:(2 * h + 1) * HALF_Q, :], preferred_element_type=F32)
        s2 = jnp.dot(keys_ref[2 * h + 1], qt[(2 * h + 1) * HALF_Q:(2 * h + 2) * HALF_Q, :],
                     preferred_element_type=F32)
        v1 = _top16(s1)
        v2 = _top16(s2)
        rank2 = _rank_among(s2, v2)
        v2lo = v2[0:8, :]
        pieces = [v1 + v2[0:1, :],
                  v1[0:8, :] + v2[1:2, :],
                  jnp.where(row16 >= 2, v1[0:1, :] + v2, NEG_INF),
                  jnp.where(row8 >= 2, v1[1:2, :] + v2lo, NEG_INF),
                  jnp.where((row8 >= 2) & (row8 <= 4), v1[2:3, :] + v2lo, NEG_INF),
                  jnp.where((row8 >= 2) & (row8 <= 3), v1[3:4, :] + v2lo, NEG_INF),
                  jnp.where(row8 == 2, v1[4:5, :] + v2lo, NEG_INF)]
        cand = jnp.concatenate(pieces, axis=0)
        tau = _top16(cand)[PEER_TOPK - 1:PEER_TOPK, :]
        sel = cand >= tau
        top = v1[0:1, :] + v2[0:1, :]
        z = jnp.sum(jnp.where(sel, jnp.exp(cand - top), 0.0), axis=0, keepdims=True)
        self32 = jnp.where(sel, 1.0, 0.0)
        per_row = self32[0:16] + jnp.concatenate([self32[16:24], jnp.zeros((8, tb), F32)], axis=0)
        off = 24
        for k1, rows in enumerate((16, 8, 8, 8, 8)):
            per_row = per_row + jnp.where(row16 == k1, jnp.sum(self32[off:off + rows], axis=0, keepdims=True), 0.0)
            off += rows
        cnt = jnp.zeros(s1.shape, F32)
        for k1 in range(PEER_TOPK):
            cnt = jnp.where(s1 == v1[k1:k1 + 1, :], per_row[k1:k1 + 1, :], cnt)
        rank2_ref[h] = rank2.astype(BF16)
        b_ref[h] = jnp.exp(s2 - v2[0:1, :]).astype(BF16)
        cnt_ref[h] = cnt
        a_ref[h] = jnp.exp(s1 - v1[0:1, :]) * (SQRT_HALF / z)


def _peer_route(xb, wqt, keys, *, tb=512):
    T, D = xb.shape
    tab_spec = pl.BlockSpec((PEER_HEADS, N_KEYS, tb), lambda i: (0, 0, i))
    return pl.pallas_call(
        _peer_route_kernel,
        grid=(T // tb,),
        in_specs=[pl.BlockSpec((tb, D), lambda i: (i, 0)),
                  pl.BlockSpec(wqt.shape, lambda i: (0, 0)),
                  pl.BlockSpec(keys.shape, lambda i: (0, 0, 0))],
        out_specs=[tab_spec] * 4,
        out_shape=[jax.ShapeDtypeStruct((PEER_HEADS, N_KEYS, T), dt) for dt in (BF16, BF16, F32, F32)],
        compiler_params=_params(("parallel",)),
        name="peer_route",
    )(xb, wqt, keys)


def _peer_expert_kernel(h_ref, x_ref, u_ref, vt_ref, rank2_ref, b_ref, cnt_ref, a_ref, g_ref, beta_ref,
                        o_ref, ob_ref, acc_ref, hid_ref, *, alpha):
    e = pl.program_id(1)

    @pl.when(e == 0)
    def _():
        acc_ref[...] = jnp.zeros_like(acc_ref)

    tb = x_ref.shape[0]
    pack = 16
    zero = jnp.zeros((N_KEYS // pack, pack, tb), BF16)

    def row(ref, h, ii):
        return jnp.broadcast_to(ref[h, ii:ii + 1, :], (pack, tb)).astype(BF16)[None]

    act = _nt_dot(u_ref[...], x_ref[...])
    n_i = u_ref.shape[0] // N_KEYS
    per = n_i // 2
    csum = None
    for ii in range(n_i):
        gate = zero
        for h in range(PEER_HEADS):
            sel = rank2_ref[h].reshape(zero.shape) < row(cnt_ref, h, ii)
            gate = gate + jnp.where(sel, b_ref[h].reshape(zero.shape), zero) * row(a_ref, h, ii)
        a = act[ii * N_KEYS:(ii + 1) * N_KEYS, :]
        hid = gate.reshape(N_KEYS, tb) * (a * (1.0 + lax.erf(a))).astype(BF16)
        hid_ref[ii * N_KEYS:(ii + 1) * N_KEYS, :] = hid
        if (ii + 1) % per == 0:
            rows = slice((ii + 1 - per) * N_KEYS, (ii + 1) * N_KEYS)
            part = jnp.dot(vt_ref[:, rows], hid_ref[rows, :], preferred_element_type=F32)
            csum = part if csum is None else csum + part
    acc_ref[...] += csum

    @pl.when(e == pl.num_programs(1) - 1)
    def _():
        y = acc_ref[...].T
        out = _layer_norm(alpha * h_ref[...] + y, g_ref[...], beta_ref[...])
        o_ref[...] = out
        ob_ref[...] = out.astype(BF16)


def _peer_expert(h, xb, ub, vtb, tables, g, beta, *, alpha, tb=512, eb=2048):
    T, D = h.shape
    E = ub.shape[0]
    rank2, bexp, cnt, a = tables
    n_i = eb // N_KEYS
    full_tab = pl.BlockSpec((PEER_HEADS, N_KEYS, tb), lambda t, e: (0, 0, t))
    row_tab = pl.BlockSpec((PEER_HEADS, n_i, tb), lambda t, e: (0, e, t))
    return pl.pallas_call(
        functools.partial(_peer_expert_kernel, alpha=alpha),
        grid=(T // tb, E // eb),
        in_specs=[pl.BlockSpec((tb, D), lambda t, e: (t, 0)),
                  pl.BlockSpec((tb, D), lambda t, e: (t, 0)),
                  pl.BlockSpec((eb, D), lambda t, e: (e, 0)),
                  pl.BlockSpec((D, eb), lambda t, e: (0, e)),
                  full_tab, full_tab, row_tab, row_tab,
                  pl.BlockSpec((1, D), lambda t, e: (0, 0)),
                  pl.BlockSpec((1, D), lambda t, e: (0, 0))],
        out_specs=[pl.BlockSpec((tb, D), lambda t, e: (t, 0)),
                   pl.BlockSpec((tb, D), lambda t, e: (t, 0))],
        out_shape=[jax.ShapeDtypeStruct((T, D), F32), jax.ShapeDtypeStruct((T, D), BF16)],
        scratch_shapes=[pltpu.VMEM((D, tb), F32), pltpu.VMEM((eb, tb), BF16)],
        compiler_params=_params(("parallel", "arbitrary")),
        name="peer_expert",
    )(h, xb, ub, vtb, rank2, bexp, cnt, a, g, beta)


def kernel(x, w_in, lam_q1, lam_k1, lam_q2, lam_k2, subln_g, w_o, ln1_g, ln1_b, w_query, sub_keys, expert_u,
           expert_v, ln2_g, ln2_b):
    B, S, D = x.shape
    T = B * S
    depth = w_in.shape[0]
    alpha = (2.0 * depth) ** 0.25
    slopes = jnp.asarray([2.0 ** (-8.0 * (i + 1) / DIFF_HEADS) for i in range(DIFF_HEADS)], F32)

    h = x.reshape(T, D)
    hb = h.astype(BF16)
    for l in range(depth):
        lambda_init = 0.8 - 0.6 * math.exp(-0.3 * l)
        lam = jnp.exp(jnp.sum(lam_q1[l] * lam_k1[l])) - jnp.exp(jnp.sum(lam_q2[l] * lam_k2[l])) + lambda_init
        scalars = jnp.stack([lam, jnp.asarray(1.0 - lambda_init, F32)]).astype(F32)

        proj = _in_proj(hb, w_in[l].astype(BF16)).reshape(B, S, -1)
        diff = _diff_attn(proj, scalars, slopes, subln_g[l].reshape(1, SLOT))
        sb = _sb_attn(proj)
        h, hb = _out_proj_ln(h, diff.reshape(T, ATT_WIDTH), sb.reshape(T, ATT_WIDTH), w_o[l].astype(BF16),
                             ln1_g[l].reshape(1, D), ln1_b[l].reshape(1, D), alpha=alpha)

        keys = sub_keys[l].reshape(2 * PEER_HEADS, N_KEYS, HALF_Q).astype(BF16)
        tables = _peer_route(hb, w_query[l].T.astype(BF16), keys)
        h, hb = _peer_expert(h, hb, (expert_u[l] * SQRT_HALF).astype(BF16), expert_v[l].T.astype(BF16), tables,
                             ln2_g[l].reshape(1, D), ln2_b[l].reshape(1, D), alpha=alpha)
    return h.reshape(B, S, D)
```

```python
import functools
import math

import jax
import jax.numpy as jnp
from jax import lax
from jax.experimental import pallas as pl
from jax.experimental.pallas import tpu as pltpu

F32 = jnp.float32
BF16 = jnp.bfloat16

HEAD_DIM = 64
DIFF_HEADS = 4
SB_HEADS = 8
SLOT = 2 * HEAD_DIM
DIFF_SLOTS = DIFF_HEADS
SB_SLOTS = SB_HEADS // 2
ATT_WIDTH = DIFF_SLOTS * SLOT
QUERY_GROUPS = (0, 3)
LOG2E = math.log2(math.e)
QUERY_SCALE = HEAD_DIM ** -0.5 * LOG2E
SQRT_HALF = math.sqrt(0.5)

PEER_HEADS = 8
N_KEYS = 128
PEER_TOPK = 16
HALF_Q = 128

LN_EPS = 1e-5
RMS_EPS = 1e-5

VMEM_LIMIT = 56 * 1024 * 1024
NEG_INF = float("-inf")


def _params(sem):
    return pltpu.CompilerParams(dimension_semantics=sem, vmem_limit_bytes=VMEM_LIMIT)


def _in_proj_kernel(x_ref, w_ref, o_ref, *, tn):
    x = x_ref[...]
    for n0 in range(0, o_ref.shape[1], tn):
        y = jnp.dot(x, w_ref[:, n0:n0 + tn], preferred_element_type=F32)
        if n0 // ATT_WIDTH in QUERY_GROUPS:
            y = y * QUERY_SCALE
        o_ref[:, n0:n0 + tn] = y.astype(o_ref.dtype)


def _in_proj(xb, wb, *, tm=512, tn=ATT_WIDTH):
    T, K = xb.shape
    N = wb.shape[1]
    return pl.pallas_call(
        functools.partial(_in_proj_kernel, tn=tn),
        grid=(T // tm,),
        in_specs=[pl.BlockSpec((tm, K), lambda i: (i, 0)),
                  pl.BlockSpec((K, N), lambda i: (0, 0))],
        out_specs=pl.BlockSpec((tm, N), lambda i: (i, 0)),
        out_shape=jax.ShapeDtypeStruct((T, N), BF16),
        compiler_params=_params(("parallel",)),
        name="in_proj",
    )(xb, wb)


def _nt_dot(a, b):
    return lax.dot_general(a, b, (((1,), (1,)), ((), ())), preferred_element_type=F32)


def _diff_parts(sc_ref, slope_ref, q_ref, k_ref, v_ref, g_ref, o_ref, hd, qi, blk):
    lam = sc_ref[0]
    post = sc_ref[1]
    slope = slope_ref[hd]

    lane = lax.broadcasted_iota(jnp.int32, (blk, SLOT), 1)
    q = q_ref[...]
    zero = jnp.zeros_like(q)
    qs = jnp.concatenate([jnp.where(lane < HEAD_DIM, q, zero), jnp.where(lane >= HEAD_DIM, q, zero)], axis=0)

    colf = lax.broadcasted_iota(jnp.int32, (1, blk), 1).astype(F32)
    ones = jnp.ones((blk, SLOT), BF16)

    def step(j, carry, masked):
        m_old, a_old = carry
        kb = k_ref[pl.ds(pl.multiple_of(j * blk, blk), blk), :]
        vb = v_ref[pl.ds(pl.multiple_of(j * blk, blk), blk), :]
        vaug = jnp.concatenate([vb, ones], axis=1)
        bias = (slope * LOG2E) * (colf + ((j - qi) * blk).astype(F32))
        s = _nt_dot(qs, kb) + bias
        if masked:
            t = lax.broadcasted_iota(jnp.int32, (2 * blk, blk), 0) & (blk - 1)
            s = jnp.where(lax.broadcasted_iota(jnp.int32, (2 * blk, blk), 1) <= t, s, NEG_INF)
        m_new = jnp.maximum(m_old, jnp.max(s, axis=1, keepdims=True))
        alpha = jnp.exp2(m_old - m_new)
        p = jnp.exp2(s - m_new).astype(BF16)
        return m_new, alpha * a_old + jnp.dot(p, vaug, preferred_element_type=F32)

    def finish(carry):
        _, a = carry
        a1, a2 = a[:blk], a[blk:]
        d = a1[:, :SLOT] / a1[:, SLOT:] - lam * (a2[:, :SLOT] / a2[:, SLOT:])
        d = d * lax.rsqrt(jnp.mean(d * d, axis=1, keepdims=True) + RMS_EPS)
        o_ref[...] = (d * g_ref[...] * post).astype(o_ref.dtype)

    init = (jnp.full((2 * blk, 1), NEG_INF, F32), jnp.zeros((2 * blk, 2 * SLOT), F32))
    return init, step, finish


def _sb_parts(q_ref, k_ref, v_ref, o_ref, qi, tq, tk, n_part):
    nsub = tq // tk
    lane = lax.broadcasted_iota(jnp.int32, (tq, SLOT), 1)
    q = q_ref[...]
    zero = jnp.zeros_like(q)
    qs = jnp.concatenate([jnp.where(lane < HEAD_DIM, q, zero), jnp.where(lane >= HEAD_DIM, q, zero)], axis=0)

    row = lax.broadcasted_iota(jnp.int32, (tk, tk), 0)
    col = lax.broadcasted_iota(jnp.int32, (tk, tk), 1)
    half = jnp.concatenate([jnp.where(row > col, -1.0, 0.0).astype(BF16), jnp.full((tk, tk), -1.0, BF16)], axis=1)
    cum_w = jnp.concatenate([half, half], axis=0)
    sign_bit = jnp.int32(-2 ** 31)

    sub = [slice(s * tk, (s + 1) * tk) for s in range(nsub)]
    rp = 2 * tq // n_part
    parts = [slice(p * rp, (p + 1) * rp) for p in range(n_part)]

    def chunk(c, carry, masked):
        later, acc = carry
        off = pl.multiple_of(c * tq, tq)
        kb = k_ref[pl.ds(off, tq), :]
        vb = v_ref[pl.ds(off, tq), :]
        zs = [_nt_dot(qs[pr], kb) for pr in parts]
        mid = []
        for p, z in enumerate(zs):
            neg_abs = lax.bitcast_convert_type(lax.bitcast_convert_type(z, jnp.int32) | sign_bit, F32)
            sp = jnp.maximum(z, 0.0) + LOG2E * jnp.log(1.0 + jnp.exp2(neg_abs))
            ls = z - sp
            lf = sp
            strict = None
            if masked:
                t = (lax.broadcasted_iota(jnp.int32, z.shape, 0) + p * rp) & (tq - 1)
                strict = lax.broadcasted_iota(jnp.int32, z.shape, 1) < t
                lf = jnp.where(strict, lf, 0.0)
            hi = lf.astype(BF16)
            lo = (lf - hi.astype(F32)).astype(BF16)
            stacked = jnp.concatenate([jnp.concatenate([hi[:, c_], lo[:, c_]], axis=1) for c_ in sub], axis=0)
            cs = jnp.dot(stacked, cum_w, preferred_element_type=F32)
            mid.append((ls, cs, strict))
        later_out, acc_out = [], []
        for pr, (ls, cs, strict) in zip(parts, mid):
            lat = later[pr]
            ws = [None] * nsub
            for s in reversed(range(nsub)):
                cs_s = cs[s * rp:(s + 1) * rp]
                w = jnp.exp2(ls[:, sub[s]] + cs_s[:, :tk] + lat)
                if masked:
                    w = jnp.where(strict[:, sub[s]], w, 0.0)
                ws[s] = w.astype(BF16)
                lat = lat + cs_s[:, tk:]
            later_out.append(lat)
            acc_out.append(acc[pr] + jnp.dot(jnp.concatenate(ws, axis=1), vb, preferred_element_type=F32))
        return jnp.concatenate(later_out, axis=0), jnp.concatenate(acc_out, axis=0)

    def finish(carry):
        _, acc = carry
        o_ref[...] = jnp.where(lane < HEAD_DIM, acc[:tq], acc[tq:]).astype(o_ref.dtype)

    init = (jnp.zeros((2 * tq, SLOT), F32), jnp.zeros((2 * tq, SLOT), F32))
    return init, chunk, finish


def _mixer_kernel(sc_ref, slope_ref, dq_ref, dk_ref, dv_ref, g_ref, sq_ref, sk_ref, sv_ref, od_ref, os_ref, *,
                  blk, tk, n_part):
    hd = pl.program_id(1)
    qi = pl.program_id(2)
    d_init, d_step, d_finish = _diff_parts(sc_ref, slope_ref, dq_ref, dk_ref, dv_ref, g_ref, od_ref, hd, qi, blk)
    s_init, s_chunk, s_finish = _sb_parts(sq_ref, sk_ref, sv_ref, os_ref, qi, blk, tk, n_part)

    s_carry = s_chunk(qi, s_init, True)

    def body(j, carry):
        d_carry, s_carry = carry
        return d_step(j, d_carry, False), s_chunk(qi - 1 - j, s_carry, False)

    d_carry, s_carry = lax.fori_loop(0, qi, body, (d_init, s_carry))
    d_finish(d_step(qi, d_carry, True))
    s_finish(s_carry)


def _mixer(proj, scalars, slopes, subln_g, *, blk=512, tk=128, n_part=2):
    B, S, _ = proj.shape
    kblk = ATT_WIDTH // SLOT
    base = 3 * kblk

    def qspec(off):
        return pl.BlockSpec((None, blk, SLOT), lambda b, h, i: (b, i, off + h))

    def kvspec(off):
        return pl.BlockSpec((None, S, SLOT), lambda b, h, i: (b, 0, off + h))

    out = jax.ShapeDtypeStruct((B, S, ATT_WIDTH), BF16)
    return pl.pallas_call(
        functools.partial(_mixer_kernel, blk=blk, tk=tk, n_part=n_part),
        grid=(B, DIFF_SLOTS, S // blk),
        in_specs=[pl.BlockSpec(memory_space=pltpu.SMEM),
                  pl.BlockSpec(memory_space=pltpu.SMEM),
                  qspec(0), kvspec(kblk), kvspec(2 * kblk),
                  pl.BlockSpec((1, SLOT), lambda b, h, i: (0, 0)),
                  qspec(base), kvspec(base + kblk), kvspec(base + 2 * kblk)],
        out_specs=[qspec(0), qspec(0)],
        out_shape=[out, out],
        compiler_params=_params(("parallel", "parallel", "parallel")),
        name="mixer",
    )(scalars, slopes, proj, proj, proj, subln_g, proj, proj, proj)


def _layer_norm(r, g, b):
    mu = jnp.mean(r, axis=-1, keepdims=True)
    c = r - mu
    var = jnp.mean(c * c, axis=-1, keepdims=True)
    return c * lax.rsqrt(var + LN_EPS) * g + b


def _out_proj_ln_kernel(h_ref, a_ref, b_ref, w_ref, g_ref, beta_ref, o_ref, ob_ref, *, alpha):
    half = a_ref.shape[1]
    y = jnp.dot(a_ref[...], w_ref[:half, :], preferred_element_type=F32)
    y = y + jnp.dot(b_ref[...], w_ref[half:, :], preferred_element_type=F32)
    out = _layer_norm(alpha * h_ref[...] + y, g_ref[...], beta_ref[...])
    o_ref[...] = out
    ob_ref[...] = out.astype(BF16)


def _out_proj_ln(h, a, b, wb, g, beta, *, alpha, tm=512):
    T, D = h.shape
    half = a.shape[1]
    return pl.pallas_call(
        functools.partial(_out_proj_ln_kernel, alpha=alpha),
        grid=(T // tm,),
        in_specs=[pl.BlockSpec((tm, D), lambda i: (i, 0)),
                  pl.BlockSpec((tm, half), lambda i: (i, 0)),
                  pl.BlockSpec((tm, half), lambda i: (i, 0)),
                  pl.BlockSpec((2 * half, D), lambda i: (0, 0)),
                  pl.BlockSpec((1, D), lambda i: (0, 0)),
                  pl.BlockSpec((1, D), lambda i: (0, 0))],
        out_specs=[pl.BlockSpec((tm, D), lambda i: (i, 0)),
                   pl.BlockSpec((tm, D), lambda i: (i, 0))],
        out_shape=[jax.ShapeDtypeStruct((T, D), F32), jax.ShapeDtypeStruct((T, D), BF16)],
        compiler_params=_params(("parallel",)),
        name="out_proj_ln",
    )(h, a, b, wb, g, beta)


def _merge_sort_network(lo, hi):
    def merge(lo, hi, r):
        step = 2 * r
        if step < hi - lo:
            yield from merge(lo, hi, step)
            yield from merge(lo + r, hi, step)
            yield from ((i, i + r) for i in range(lo + r, hi - r, step))
        else:
            yield (lo, lo + r)
    if hi - lo >= 1:
        mid = lo + (hi - lo) // 2
        yield from _merge_sort_network(lo, mid)
        yield from _merge_sort_network(mid + 1, hi)
        yield from merge(lo, hi, 1)


def _sort_network(n):
    if n & (n - 1) == 0:
        return list(_merge_sort_network(0, n - 1))
    return [(i, i + 1) for rnd in range(n) for i in range(rnd % 2, n - 1, 2)]


def _top16(s):
    n = s.shape[0] // 8
    v = [s[8 * r:8 * (r + 1), :] for r in range(n)]
    for i, j in _sort_network(n):
        v[i], v[j] = jnp.maximum(v[i], v[j]), jnp.minimum(v[i], v[j])
    vals = []
    for k in range(PEER_TOPK):
        m = jnp.max(v[0], axis=0, keepdims=True)
        vals.append(m)
        eq = v[0] == m
        for r in range(min(n, PEER_TOPK - 1 - k)):
            v[r] = jnp.where(eq, v[r + 1] if r + 1 < n else NEG_INF, v[r])
    return jnp.concatenate(vals, axis=0)


def _rank_among(s, v):
    rank = jnp.full(s.shape, float(PEER_TOPK), F32)
    for k in reversed(range(PEER_TOPK)):
        rank = jnp.where(s == v[k:k + 1, :], float(k), rank)
    return rank


def _peer_route_kernel(x_ref, wq_ref, keys_ref, rank2_ref, b_ref, cnt_ref, a_ref):
    tb = x_ref.shape[0]
    qt = _nt_dot(wq_ref[...], x_ref[...]).astype(BF16)
    row8 = lax.broadcasted_iota(jnp.int32, (8, tb), 0)
    row16 = lax.broadcasted_iota(jnp.int32, (PEER_TOPK, tb), 0)
    for h in range(PEER_HEADS):
        s1 = jnp.dot(keys_ref[2 * h], qt[(2 * h) * HALF_Q:(2 * h + 1) * HALF_Q, :], preferred_element_type=F32)
        s2 = jnp.dot(keys_ref[2 * h + 1], qt[(2 * h + 1) * HALF_Q:(2 * h + 2) * HALF_Q, :],
                     preferred_element_type=F32)
        v1 = _top16(s1)
        v2 = _top16(s2)
        rank2 = _rank_among(s2, v2)
        v2lo = v2[0:8, :]
        pieces = [v1 + v2[0:1, :],
                  v1[0:8, :] + v2[1:2, :],
                  jnp.where(row16 >= 2, v1[0:1, :] + v2, NEG_INF),
                  jnp.where(row8 >= 2, v1[1:2, :] + v2lo, NEG_INF),
                  jnp.where((row8 >= 2) & (row8 <= 4), v1[2:3, :] + v2lo, NEG_INF),
                  jnp.where((row8 >= 2) & (row8 <= 3), v1[3:4, :] + v2lo, NEG_INF),
                  jnp.where(row8 == 2, v1[4:5, :] + v2lo, NEG_INF)]
        cand = jnp.concatenate(pieces, axis=0)
        tau = _top16(cand)[PEER_TOPK - 1:PEER_TOPK, :]
        sel = cand >= tau
        top = v1[0:1, :] + v2[0:1, :]
        z = jnp.sum(jnp.where(sel, jnp.exp(cand - top), 0.0), axis=0, keepdims=True)
        self32 = jnp.where(sel, 1.0, 0.0)
        per_row = self32[0:16] + jnp.concatenate([self32[16:24], jnp.zeros((8, tb), F32)], axis=0)
        off = 24
        for k1, rows in enumerate((16, 8, 8, 8, 8)):
            per_row = per_row + jnp.where(row16 == k1, jnp.sum(self32[off:off + rows], axis=0, keepdims=True), 0.0)
            off += rows
        cnt = jnp.zeros(s1.shape, F32)
        for k1 in range(PEER_TOPK):
            cnt = jnp.where(s1 == v1[k1:k1 + 1, :], per_row[k1:k1 + 1, :], cnt)
        rank2_ref[h] = rank2.astype(BF16)
        b_ref[h] = jnp.exp(s2 - v2[0:1, :]).astype(BF16)
        cnt_ref[h] = cnt
        a_ref[h] = jnp.exp(s1 - v1[0:1, :]) * (SQRT_HALF / z)


def _peer_route(xb, wqt, keys, *, tb=512):
    T, D = xb.shape
    tab_spec = pl.BlockSpec((PEER_HEADS, N_KEYS, tb), lambda i: (0, 0, i))
    return pl.pallas_call(
        _peer_route_kernel,
        grid=(T // tb,),
        in_specs=[pl.BlockSpec((tb, D), lambda i: (i, 0)),
                  pl.BlockSpec(wqt.shape, lambda i: (0, 0)),
                  pl.BlockSpec(keys.shape, lambda i: (0, 0, 0))],
        out_specs=[tab_spec] * 4,
        out_shape=[jax.ShapeDtypeStruct((PEER_HEADS, N_KEYS, T), dt) for dt in (BF16, BF16, F32, F32)],
        compiler_params=_params(("parallel",)),
        name="peer_route",
    )(xb, wqt, keys)


def _peer_expert_kernel(h_ref, x_ref, u_ref, vt_ref, rank2_ref, b_ref, cnt_ref, a_ref, g_ref, beta_ref,
                        o_ref, ob_ref, acc_ref, hid_ref, *, alpha):
    e = pl.program_id(1)

    @pl.when(e == 0)
    def _():
        acc_ref[...] = jnp.zeros_like(acc_ref)

    tb = x_ref.shape[0]
    pack = 16
    zero = jnp.zeros((N_KEYS // pack, pack, tb), BF16)

    def row(ref, h, ii):
        return jnp.broadcast_to(ref[h, ii:ii + 1, :], (pack, tb)).astype(BF16)[None]

    act = _nt_dot(u_ref[...], x_ref[...])
    n_i = u_ref.shape[0] // N_KEYS
    per = n_i // 2
    csum = None
    for ii in range(n_i):
        gate = zero
        for h in range(PEER_HEADS):
            sel = rank2_ref[h].reshape(zero.shape) < row(cnt_ref, h, ii)
            gate = gate + jnp.where(sel, b_ref[h].reshape(zero.shape), zero) * row(a_ref, h, ii)
        a = act[ii * N_KEYS:(ii + 1) * N_KEYS, :]
        hid = gate.reshape(N_KEYS, tb) * (a * (1.0 + lax.erf(a))).astype(BF16)
        hid_ref[ii * N_KEYS:(ii + 1) * N_KEYS, :] = hid
        if (ii + 1) % per == 0:
            rows = slice((ii + 1 - per) * N_KEYS, (ii + 1) * N_KEYS)
            part = jnp.dot(vt_ref[:, rows], hid_ref[rows, :], preferred_element_type=F32)
            csum = part if csum is None else csum + part
    acc_ref[...] += csum

    @pl.when(e == pl.num_programs(1) - 1)
    def _():
        y = acc_ref[...].T
        out = _layer_norm(alpha * h_ref[...] + y, g_ref[...], beta_ref[...])
        o_ref[...] = out
        ob_ref[...] = out.astype(BF16)


def _peer_expert(h, xb, ub, vtb, tables, g, beta, *, alpha, tb=512, eb=2048):
    T, D = h.shape
    E = ub.shape[0]
    rank2, bexp, cnt, a = tables
    n_i = eb // N_KEYS
    full_tab = pl.BlockSpec((PEER_HEADS, N_KEYS, tb), lambda t, e: (0, 0, t))
    row_tab = pl.BlockSpec((PEER_HEADS, n_i, tb), lambda t, e: (0, e, t))
    return pl.pallas_call(
        functools.partial(_peer_expert_kernel, alpha=alpha),
        grid=(T // tb, E // eb),
        in_specs=[pl.BlockSpec((tb, D), lambda t, e: (t, 0)),
                  pl.BlockSpec((tb, D), lambda t, e: (t, 0)),
                  pl.BlockSpec((eb, D), lambda t, e: (e, 0)),
                  pl.BlockSpec((D, eb), lambda t, e: (0, e)),
                  full_tab, full_tab, row_tab, row_tab,
                  pl.BlockSpec((1, D), lambda t, e: (0, 0)),
                  pl.BlockSpec((1, D), lambda t, e: (0, 0))],
        out_specs=[pl.BlockSpec((tb, D), lambda t, e: (t, 0)),
                   pl.BlockSpec((tb, D), lambda t, e: (t, 0))],
        out_shape=[jax.ShapeDtypeStruct((T, D), F32), jax.ShapeDtypeStruct((T, D), BF16)],
        scratch_shapes=[pltpu.VMEM((D, tb), F32), pltpu.VMEM((eb, tb), BF16)],
        compiler_params=_params(("parallel", "arbitrary")),
        name="peer_expert",
    )(h, xb, ub, vtb, rank2, bexp, cnt, a, g, beta)


def kernel(x, w_in, lam_q1, lam_k1, lam_q2, lam_k2, subln_g, w_o, ln1_g, ln1_b, w_query, sub_keys, expert_u,
           expert_v, ln2_g, ln2_b):
    B, S, D = x.shape
    T = B * S
    depth = w_in.shape[0]
    alpha = (2.0 * depth) ** 0.25
    slopes = jnp.asarray([2.0 ** (-8.0 * (i + 1) / DIFF_HEADS) for i in range(DIFF_HEADS)], F32)

    h = x.reshape(T, D)
    hb = h.astype(BF16)
    for l in range(depth):
        lambda_init = 0.8 - 0.6 * math.exp(-0.3 * l)
        lam = jnp.exp(jnp.sum(lam_q1[l] * lam_k1[l])) - jnp.exp(jnp.sum(lam_q2[l] * lam_k2[l])) + lambda_init
        scalars = jnp.stack([lam, jnp.asarray(1.0 - lambda_init, F32)]).astype(F32)

        proj = _in_proj(hb, w_in[l].astype(BF16)).reshape(B, S, -1)
        diff, sb = _mixer(proj, scalars, slopes, subln_g[l].reshape(1, SLOT))
        h, hb = _out_proj_ln(h, diff.reshape(T, ATT_WIDTH), sb.reshape(T, ATT_WIDTH), w_o[l].astype(BF16),
                             ln1_g[l].reshape(1, D), ln1_b[l].reshape(1, D), alpha=alpha)

        keys = sub_keys[l].reshape(2 * PEER_HEADS, N_KEYS, HALF_Q).astype(BF16)
        tables = _peer_route(hb, w_query[l].T.astype(BF16), keys)
        h, hb = _peer_expert(h, hb, (expert_u[l] * SQRT_HALF).astype(BF16), expert_v[l].T.astype(BF16), tables,
                             ln2_g[l].reshape(1, D), ln2_b[l].reshape(1, D), alpha=alpha)
    return h.reshape(B, S, D)
```

```python
import functools
import math

import jax
import jax.numpy as jnp
from jax import lax
from jax.experimental import pallas as pl
from jax.experimental.pallas import tpu as pltpu

F32 = jnp.float32
BF16 = jnp.bfloat16

HEAD_DIM = 64
DIFF_HEADS = 4
SB_HEADS = 8
SLOT = 2 * HEAD_DIM
DIFF_SLOTS = DIFF_HEADS
SB_SLOTS = SB_HEADS // 2
ATT_WIDTH = DIFF_SLOTS * SLOT
QUERY_GROUPS = (0, 3)
LOG2E = math.log2(math.e)
QUERY_SCALE = HEAD_DIM ** -0.5 * LOG2E
SQRT_HALF = math.sqrt(0.5)

PEER_HEADS = 8
N_KEYS = 128
PEER_TOPK = 16
HALF_Q = 128

LN_EPS = 1e-5
RMS_EPS = 1e-5

VMEM_LIMIT = 56 * 1024 * 1024
NEG_INF = float("-inf")


def _params(sem):
    return pltpu.CompilerParams(dimension_semantics=sem, vmem_limit_bytes=VMEM_LIMIT)


def _in_proj_kernel(x_ref, w_ref, o_ref, *, tn):
    x = x_ref[...]
    for n0 in range(0, o_ref.shape[1], tn):
        y = jnp.dot(x, w_ref[:, n0:n0 + tn], preferred_element_type=F32)
        if n0 // ATT_WIDTH in QUERY_GROUPS:
            y = y * QUERY_SCALE
        o_ref[:, n0:n0 + tn] = y.astype(o_ref.dtype)


def _in_proj(xb, wb, *, tm=512, tn=ATT_WIDTH):
    T, K = xb.shape
    N = wb.shape[1]
    return pl.pallas_call(
        functools.partial(_in_proj_kernel, tn=tn),
        grid=(T // tm,),
        in_specs=[pl.BlockSpec((tm, K), lambda i: (i, 0)),
                  pl.BlockSpec((K, N), lambda i: (0, 0))],
        out_specs=pl.BlockSpec((tm, N), lambda i: (i, 0)),
        out_shape=jax.ShapeDtypeStruct((T, N), BF16),
        compiler_params=_params(("parallel",)),
        name="in_proj",
    )(xb, wb)


def _nt_dot(a, b):
    return lax.dot_general(a, b, (((1,), (1,)), ((), ())), preferred_element_type=F32)


def _diff_parts(sc_ref, slope_ref, q_ref, k_ref, v_ref, g_ref, o_ref, hd, qi, blk):
    lam = sc_ref[0]
    post = sc_ref[1]
    slope = slope_ref[hd]

    lane = lax.broadcasted_iota(jnp.int32, (blk, SLOT), 1)
    q = q_ref[...]
    zero = jnp.zeros_like(q)
    qs = jnp.concatenate([jnp.where(lane < HEAD_DIM, q, zero), jnp.where(lane >= HEAD_DIM, q, zero)], axis=0)

    colf = lax.broadcasted_iota(jnp.int32, (1, blk), 1).astype(F32)
    ones = jnp.ones((blk, SLOT), BF16)

    def step(j, carry, masked):
        m_old, a_old = carry
        kb = k_ref[pl.ds(pl.multiple_of(j * blk, blk), blk), :]
        vb = v_ref[pl.ds(pl.multiple_of(j * blk, blk), blk), :]
        vaug = jnp.concatenate([vb, ones], axis=1)
        bias = (slope * LOG2E) * (colf + ((j - qi) * blk).astype(F32))
        s = _nt_dot(qs, kb) + bias
        if masked:
            t = lax.broadcasted_iota(jnp.int32, (2 * blk, blk), 0) & (blk - 1)
            s = jnp.where(lax.broadcasted_iota(jnp.int32, (2 * blk, blk), 1) <= t, s, NEG_INF)
        m_new = jnp.maximum(m_old, jnp.max(s, axis=1, keepdims=True))
        alpha = jnp.exp2(m_old - m_new)
        p = jnp.exp2(s - m_new).astype(BF16)
        return m_new, alpha * a_old + jnp.dot(p, vaug, preferred_element_type=F32)

    def finish(carry):
        _, a = carry
        a1, a2 = a[:blk], a[blk:]
        d = a1[:, :SLOT] / a1[:, SLOT:] - lam * (a2[:, :SLOT] / a2[:, SLOT:])
        d = d * lax.rsqrt(jnp.mean(d * d, axis=1, keepdims=True) + RMS_EPS)
        o_ref[...] = (d * g_ref[...] * post).astype(o_ref.dtype)

    init = (jnp.full((2 * blk, 1), NEG_INF, F32), jnp.zeros((2 * blk, 2 * SLOT), F32))
    return init, step, finish


def _sb_parts(q_ref, k_ref, v_ref, o_ref, qi, tq, tk, n_part):
    nsub = tq // tk
    lane = lax.broadcasted_iota(jnp.int32, (tq, SLOT), 1)
    q = q_ref[...]
    zero = jnp.zeros_like(q)
    qs = jnp.concatenate([jnp.where(lane < HEAD_DIM, q, zero), jnp.where(lane >= HEAD_DIM, q, zero)], axis=0)

    row = lax.broadcasted_iota(jnp.int32, (tk, tk), 0)
    col = lax.broadcasted_iota(jnp.int32, (tk, tk), 1)
    half = jnp.concatenate([jnp.where(row > col, -1.0, 0.0).astype(BF16), jnp.full((tk, tk), -1.0, BF16)], axis=1)
    cum_w = jnp.concatenate([half, half], axis=0)
    sign_bit = jnp.int32(-2 ** 31)

    sub = [slice(s * tk, (s + 1) * tk) for s in range(nsub)]
    rp = 2 * tq // n_part
    parts = [slice(p * rp, (p + 1) * rp) for p in range(n_part)]

    def chunk(c, carry, masked):
        later, acc = carry
        off = pl.multiple_of(c * tq, tq)
        kb = k_ref[pl.ds(off, tq), :]
        vb = v_ref[pl.ds(off, tq), :]
        zs = [_nt_dot(qs[pr], kb) for pr in parts]
        mid = []
        for p, z in enumerate(zs):
            neg_abs = lax.bitcast_convert_type(lax.bitcast_convert_type(z, jnp.int32) | sign_bit, F32)
            sp = jnp.maximum(z, 0.0) + LOG2E * jnp.log(1.0 + jnp.exp2(neg_abs))
            ls = z - sp
            lf = sp
            strict = None
            if masked:
                t = (lax.broadcasted_iota(jnp.int32, z.shape, 0) + p * rp) & (tq - 1)
                strict = lax.broadcasted_iota(jnp.int32, z.shape, 1) < t
                lf = jnp.where(strict, lf, 0.0)
            hi = lf.astype(BF16)
            lo = (lf - hi.astype(F32)).astype(BF16)
            stacked = jnp.concatenate([jnp.concatenate([hi[:, c_], lo[:, c_]], axis=1) for c_ in sub], axis=0)
            cs = jnp.dot(stacked, cum_w, preferred_element_type=F32)
            mid.append((ls, cs, strict))
        later_out, acc_out = [], []
        for pr, (ls, cs, strict) in zip(parts, mid):
            lat = later[pr]
            ws = [None] * nsub
            for s in reversed(range(nsub)):
                cs_s = cs[s * rp:(s + 1) * rp]
                w = jnp.exp2(ls[:, sub[s]] + cs_s[:, :tk] + lat)
                if masked:
                    w = jnp.where(strict[:, sub[s]], w, 0.0)
                ws[s] = w.astype(BF16)
                lat = lat + cs_s[:, tk:]
            later_out.append(lat)
            acc_out.append(acc[pr] + jnp.dot(jnp.concatenate(ws, axis=1), vb, preferred_element_type=F32))
        return jnp.concatenate(later_out, axis=0), jnp.concatenate(acc_out, axis=0)

    def finish(carry):
        _, acc = carry
        o_ref[...] = jnp.where(lane < HEAD_DIM, acc[:tq], acc[tq:]).astype(o_ref.dtype)

    init = (jnp.zeros((2 * tq, SLOT), F32), jnp.zeros((2 * tq, SLOT), F32))
    return init, chunk, finish


def _mixer_kernel(sc_ref, slope_ref, dq_ref, dk_ref, dv_ref, g_ref, sq_ref, sk_ref, sv_ref, od_ref, os_ref, *,
                  blk, tk, n_part):
    hd = pl.program_id(1)
    qi = pl.program_id(2)
    d_init, d_step, d_finish = _diff_parts(sc_ref, slope_ref, dq_ref, dk_ref, dv_ref, g_ref, od_ref, hd, qi, blk)
    s_init, s_chunk, s_finish = _sb_parts(sq_ref, sk_ref, sv_ref, os_ref, qi, blk, tk, n_part)

    s_carry = s_chunk(qi, s_init, True)
    d_carry = d_step(qi, d_init, True)

    def body(j, carry):
        d_carry, s_carry = carry
        return d_step(j, d_carry, False), s_chunk(qi - 1 - j, s_carry, False)

    d_carry, s_carry = lax.fori_loop(0, qi, body, (d_carry, s_carry))
    d_finish(d_carry)
    s_finish(s_carry)


def _mixer(proj, scalars, slopes, subln_g, *, blk=512, tk=128, n_part=2):
    B, S, _ = proj.shape
    kblk = ATT_WIDTH // SLOT
    base = 3 * kblk

    def qspec(off):
        return pl.BlockSpec((None, blk, SLOT), lambda b, h, i: (b, i, off + h))

    def kvspec(off):
        return pl.BlockSpec((None, S, SLOT), lambda b, h, i: (b, 0, off + h))

    out = jax.ShapeDtypeStruct((B, S, ATT_WIDTH), BF16)
    return pl.pallas_call(
        functools.partial(_mixer_kernel, blk=blk, tk=tk, n_part=n_part),
        grid=(B, DIFF_SLOTS, S // blk),
        in_specs=[pl.BlockSpec(memory_space=pltpu.SMEM),
                  pl.BlockSpec(memory_space=pltpu.SMEM),
                  qspec(0), kvspec(kblk), kvspec(2 * kblk),
                  pl.BlockSpec((1, SLOT), lambda b, h, i: (0, 0)),
                  qspec(base), kvspec(base + kblk), kvspec(base + 2 * kblk)],
        out_specs=[qspec(0), qspec(0)],
        out_shape=[out, out],
        compiler_params=_params(("parallel", "parallel", "parallel")),
        name="mixer",
    )(scalars, slopes, proj, proj, proj, subln_g, proj, proj, proj)


def _layer_norm(r, g, b):
    mu = jnp.mean(r, axis=-1, keepdims=True)
    c = r - mu
    var = jnp.mean(c * c, axis=-1, keepdims=True)
    return c * lax.rsqrt(var + LN_EPS) * g + b


def _out_proj_ln_kernel(h_ref, a_ref, b_ref, w_ref, g_ref, beta_ref, o_ref, ob_ref, *, alpha):
    half = a_ref.shape[1]
    y = jnp.dot(a_ref[...], w_ref[:half, :], preferred_element_type=F32)
    y = y + jnp.dot(b_ref[...], w_ref[half:, :], preferred_element_type=F32)
    out = _layer_norm(alpha * h_ref[...] + y, g_ref[...], beta_ref[...])
    o_ref[...] = out
    ob_ref[...] = out.astype(BF16)


def _out_proj_ln(h, a, b, wb, g, beta, *, alpha, tm=512):
    T, D = h.shape
    half = a.shape[1]
    return pl.pallas_call(
        functools.partial(_out_proj_ln_kernel, alpha=alpha),
        grid=(T // tm,),
        in_specs=[pl.BlockSpec((tm, D), lambda i: (i, 0)),
                  pl.BlockSpec((tm, half), lambda i: (i, 0)),
                  pl.BlockSpec((tm, half), lambda i: (i, 0)),
                  pl.BlockSpec((2 * half, D), lambda i: (0, 0)),
                  pl.BlockSpec((1, D), lambda i: (0, 0)),
                  pl.BlockSpec((1, D), lambda i: (0, 0))],
        out_specs=[pl.BlockSpec((tm, D), lambda i: (i, 0)),
                   pl.BlockSpec((tm, D), lambda i: (i, 0))],
        out_shape=[jax.ShapeDtypeStruct((T, D), F32), jax.ShapeDtypeStruct((T, D), BF16)],
        compiler_params=_params(("parallel",)),
        name="out_proj_ln",
    )(h, a, b, wb, g, beta)


def _merge_sort_network(lo, hi):
    def merge(lo, hi, r):
        step = 2 * r
        if step < hi - lo:
            yield from merge(lo, hi, step)
            yield from merge(lo + r, hi, step)
            yield from ((i, i + r) for i in range(lo + r, hi - r, step))
        else:
            yield (lo, lo + r)
    if hi - lo >= 1:
        mid = lo + (hi - lo) // 2
        yield from _merge_sort_network(lo, mid)
        yield from _merge_sort_network(mid + 1, hi)
        yield from merge(lo, hi, 1)


def _sort_network(n):
    if n & (n - 1) == 0:
        return list(_merge_sort_network(0, n - 1))
    return [(i, i + 1) for rnd in range(n) for i in range(rnd % 2, n - 1, 2)]


def _top16(s):
    n = s.shape[0] // 8
    v = [s[8 * r:8 * (r + 1), :] for r in range(n)]
    for i, j in _sort_network(n):
        v[i], v[j] = jnp.maximum(v[i], v[j]), jnp.minimum(v[i], v[j])
    vals = []
    for k in range(PEER_TOPK):
        m = jnp.max(v[0], axis=0, keepdims=True)
        vals.append(m)
        eq = v[0] == m
        for r in range(min(n, PEER_TOPK - 1 - k)):
            v[r] = jnp.where(eq, v[r + 1] if r + 1 < n else NEG_INF, v[r])
    return jnp.concatenate(vals, axis=0)


def _rank_among(s, v):
    rank = jnp.full(s.shape, float(PEER_TOPK), F32)
    for k in reversed(range(PEER_TOPK)):
        rank = jnp.where(s == v[k:k + 1, :], float(k), rank)
    return rank


def _peer_route_kernel(x_ref, wq_ref, keys_ref, rank2_ref, b_ref, cnt_ref, a_ref):
    tb = x_ref.shape[0]
    qt = _nt_dot(wq_ref[...], x_ref[...]).astype(BF16)
    row8 = lax.broadcasted_iota(jnp.int32, (8, tb), 0)
    row16 = lax.broadcasted_iota(jnp.int32, (PEER_TOPK, tb), 0)
    for h in range(PEER_HEADS):
        s1 = jnp.dot(keys_ref[2 * h], qt[(2 * h) * HALF_Q:(2 * h + 1) * HALF_Q, :], preferred_element_type=F32)
        s2 = jnp.dot(keys_ref[2 * h + 1], qt[(2 * h + 1) * HALF_Q:(2 * h + 2) * HALF_Q, :],
                     preferred_element_type=F32)
        v1 = _top16(s1)
        v2 = _top16(s2)
        rank2 = _rank_among(s2, v2)
        v2lo = v2[0:8, :]
        pieces = [v1 + v2[0:1, :],
                  v1[0:8, :] + v2[1:2, :],
                  jnp.where(row16 >= 2, v1[0:1, :] + v2, NEG_INF),
                  jnp.where(row8 >= 2, v1[1:2, :] + v2lo, NEG_INF),
                  jnp.where((row8 >= 2) & (row8 <= 4), v1[2:3, :] + v2lo, NEG_INF),
                  jnp.where((row8 >= 2) & (row8 <= 3), v1[3:4, :] + v2lo, NEG_INF),
                  jnp.where(row8 == 2, v1[4:5, :] + v2lo, NEG_INF)]
        cand = jnp.concatenate(pieces, axis=0)
        tau = _top16(cand)[PEER_TOPK - 1:PEER_TOPK, :]
        sel = cand >= tau
        top = v1[0:1, :] + v2[0:1, :]
        z = jnp.sum(jnp.where(sel, jnp.exp(cand - top), 0.0), axis=0, keepdims=True)
        self32 = jnp.where(sel, 1.0, 0.0)
        per_row = self32[0:16] + jnp.concatenate([self32[16:24], jnp.zeros((8, tb), F32)], axis=0)
        off = 24
        for k1, rows in enumerate((16, 8, 8, 8, 8)):
            per_row = per_row + jnp.where(row16 == k1, jnp.sum(self32[off:off + rows], axis=0, keepdims=True), 0.0)
            off += rows
        cnt = jnp.zeros(s1.shape, F32)
        for k1 in range(PEER_TOPK):
            cnt = jnp.where(s1 == v1[k1:k1 + 1, :], per_row[k1:k1 + 1, :], cnt)
        rank2_ref[h] = rank2.astype(BF16)
        b_ref[h] = jnp.exp(s2 - v2[0:1, :]).astype(BF16)
        cnt_ref[h] = cnt
        a_ref[h] = jnp.exp(s1 - v1[0:1, :]) * (SQRT_HALF / z)


def _peer_route(xb, wqt, keys, *, tb=512):
    T, D = xb.shape
    tab_spec = pl.BlockSpec((PEER_HEADS, N_KEYS, tb), lambda i: (0, 0, i))
    return pl.pallas_call(
        _peer_route_kernel,
        grid=(T // tb,),
        in_specs=[pl.BlockSpec((tb, D), lambda i: (i, 0)),
                  pl.BlockSpec(wqt.shape, lambda i: (0, 0)),
                  pl.BlockSpec(keys.shape, lambda i: (0, 0, 0))],
        out_specs=[tab_spec] * 4,
        out_shape=[jax.ShapeDtypeStruct((PEER_HEADS, N_KEYS, T), dt) for dt in (BF16, BF16, F32, F32)],
        compiler_params=_params(("parallel",)),
        name="peer_route",
    )(xb, wqt, keys)


def _peer_expert_kernel(h_ref, x_ref, u_ref, vt_ref, rank2_ref, b_ref, cnt_ref, a_ref, g_ref, beta_ref,
                        o_ref, ob_ref, acc_ref, hid_ref, *, alpha):
    e = pl.program_id(1)

    @pl.when(e == 0)
    def _():
        acc_ref[...] = jnp.zeros_like(acc_ref)

    tb = x_ref.shape[0]
    pack = 16
    zero = jnp.zeros((N_KEYS // pack, pack, tb), BF16)

    def row(ref, h, ii):
        return jnp.broadcast_to(ref[h, ii:ii + 1, :], (pack, tb)).astype(BF16)[None]

    act = _nt_dot(u_ref[...], x_ref[...])
    n_i = u_ref.shape[0] // N_KEYS
    per = n_i // 2
    csum = None
    for ii in range(n_i):
        gate = None
        for h in range(PEER_HEADS):
            sel = rank2_ref[h].reshape(zero.shape) < row(cnt_ref, h, ii)
            term = jnp.where(sel, b_ref[h].reshape(zero.shape), zero) * row(a_ref, h, ii)
            gate = term if gate is None else gate + term
        a = act[ii * N_KEYS:(ii + 1) * N_KEYS, :]
        hid = gate.reshape(N_KEYS, tb) * (a * (1.0 + lax.erf(a))).astype(BF16)
        hid_ref[ii * N_KEYS:(ii + 1) * N_KEYS, :] = hid
        if (ii + 1) % per == 0:
            rows = slice((ii + 1 - per) * N_KEYS, (ii + 1) * N_KEYS)
            part = jnp.dot(vt_ref[:, rows], hid_ref[rows, :], preferred_element_type=F32)
            csum = part if csum is None else csum + part
    acc_ref[...] += csum

    @pl.when(e == pl.num_programs(1) - 1)
    def _():
        y = acc_ref[...].T
        out = _layer_norm(alpha * h_ref[...] + y, g_ref[...], beta_ref[...])
        o_ref[...] = out
        ob_ref[...] = out.astype(BF16)


def _peer_expert(h, xb, ub, vtb, tables, g, beta, *, alpha, tb=512, eb=2048):
    T, D = h.shape
    E = ub.shape[0]
    rank2, bexp, cnt, a = tables
    n_i = eb // N_KEYS
    full_tab = pl.BlockSpec((PEER_HEADS, N_KEYS, tb), lambda t, e: (0, 0, t))
    row_tab = pl.BlockSpec((PEER_HEADS, n_i, tb), lambda t, e: (0, e, t))
    return pl.pallas_call(
        functools.partial(_peer_expert_kernel, alpha=alpha),
        grid=(T // tb, E // eb),
        in_specs=[pl.BlockSpec((tb, D), lambda t, e: (t, 0)),
                  pl.BlockSpec((tb, D), lambda t, e: (t, 0)),
                  pl.BlockSpec((eb, D), lambda t, e: (e, 0)),
                  pl.BlockSpec((D, eb), lambda t, e: (0, e)),
                  full_tab, full_tab, row_tab, row_tab,
                  pl.BlockSpec((1, D), lambda t, e: (0, 0)),
                  pl.BlockSpec((1, D), lambda t, e: (0, 0))],
        out_specs=[pl.BlockSpec((tb, D), lambda t, e: (t, 0)),
                   pl.BlockSpec((tb, D), lambda t, e: (t, 0))],
        out_shape=[jax.ShapeDtypeStruct((T, D), F32), jax.ShapeDtypeStruct((T, D), BF16)],
        scratch_shapes=[pltpu.VMEM((D, tb), F32), pltpu.VMEM((eb, tb), BF16)],
        compiler_params=_params(("parallel", "arbitrary")),
        name="peer_expert",
    )(h, xb, ub, vtb, rank2, bexp, cnt, a, g, beta)


def kernel(x, w_in, lam_q1, lam_k1, lam_q2, lam_k2, subln_g, w_o, ln1_g, ln1_b, w_query, sub_keys, expert_u,
           expert_v, ln2_g, ln2_b):
    B, S, D = x.shape
    T = B * S
    depth = w_in.shape[0]
    alpha = (2.0 * depth) ** 0.25
    slopes = jnp.asarray([2.0 ** (-8.0 * (i + 1) / DIFF_HEADS) for i in range(DIFF_HEADS)], F32)

    h = x.reshape(T, D)
    hb = h.astype(BF16)
    for l in range(depth):
        lambda_init = 0.8 - 0.6 * math.exp(-0.3 * l)
        lam = jnp.exp(jnp.sum(lam_q1[l] * lam_k1[l])) - jnp.exp(jnp.sum(lam_q2[l] * lam_k2[l])) + lambda_init
        scalars = jnp.stack([lam, jnp.asarray(1.0 - lambda_init, F32)]).astype(F32)

        proj = _in_proj(hb, w_in[l].astype(BF16)).reshape(B, S, -1)
        diff, sb = _mixer(proj, scalars, slopes, subln_g[l].reshape(1, SLOT))
        h, hb = _out_proj_ln(h, diff.reshape(T, ATT_WIDTH), sb.reshape(T, ATT_WIDTH), w_o[l].astype(BF16),
                             ln1_g[l].reshape(1, D), ln1_b[l].reshape(1, D), alpha=alpha)

        keys = sub_keys[l].reshape(2 * PEER_HEADS, N_KEYS, HALF_Q).astype(BF16)
        tables = _peer_route(hb, w_query[l].T.astype(BF16), keys)
        h, hb = _peer_expert(h, hb, (expert_u[l] * SQRT_HALF).astype(BF16), expert_v[l].T.astype(BF16), tables,
                             ln2_g[l].reshape(1, D), ln2_b[l].reshape(1, D), alpha=alpha)
    return h.reshape(B, S, D)
```

```python
import functools
import math

import jax
import jax.numpy as jnp
from jax import lax
from jax.experimental import pallas as pl
from jax.experimental.pallas import tpu as pltpu

F32 = jnp.float32
BF16 = jnp.bfloat16

HEAD_DIM = 64
DIFF_HEADS = 4
SB_HEADS = 8
SLOT = 2 * HEAD_DIM
DIFF_SLOTS = DIFF_HEADS
SB_SLOTS = SB_HEADS // 2
ATT_WIDTH = DIFF_SLOTS * SLOT
QUERY_GROUPS = (0, 3)
LOG2E = math.log2(math.e)
QUERY_SCALE = HEAD_DIM ** -0.5 * LOG2E
SQRT_HALF = math.sqrt(0.5)

PEER_HEADS = 8
N_KEYS = 128
PEER_TOPK = 16
HALF_Q = 128

LN_EPS = 1e-5
RMS_EPS = 1e-5

VMEM_LIMIT = 56 * 1024 * 1024
NEG_INF = float("-inf")


def _params(sem):
    return pltpu.CompilerParams(dimension_semantics=sem, vmem_limit_bytes=VMEM_LIMIT)


def _in_proj_kernel(x_ref, w_ref, o_ref, *, tn):
    x = x_ref[...]
    for n0 in range(0, o_ref.shape[1], tn):
        y = jnp.dot(x, w_ref[:, n0:n0 + tn], preferred_element_type=F32)
        if n0 // ATT_WIDTH in QUERY_GROUPS:
            y = y * QUERY_SCALE
        o_ref[:, n0:n0 + tn] = y.astype(o_ref.dtype)


def _in_proj(xb, wb, *, tm=512, tn=ATT_WIDTH):
    T, K = xb.shape
    N = wb.shape[1]
    return pl.pallas_call(
        functools.partial(_in_proj_kernel, tn=tn),
        grid=(T // tm,),
        in_specs=[pl.BlockSpec((tm, K), lambda i: (i, 0)),
                  pl.BlockSpec((K, N), lambda i: (0, 0))],
        out_specs=pl.BlockSpec((tm, N), lambda i: (i, 0)),
        out_shape=jax.ShapeDtypeStruct((T, N), BF16),
        compiler_params=_params(("parallel",)),
        name="in_proj",
    )(xb, wb)


def _nt_dot(a, b):
    return lax.dot_general(a, b, (((1,), (1,)), ((), ())), preferred_element_type=F32)


def _diff_parts(sc_ref, slope_ref, q_ref, k_ref, v_ref, g_ref, o_ref, hd, qi, blk):
    lam = sc_ref[0]
    post = sc_ref[1]
    slope = slope_ref[hd]

    lane = lax.broadcasted_iota(jnp.int32, (blk, SLOT), 1)
    q = q_ref[...]
    zero = jnp.zeros_like(q)
    qs = jnp.concatenate([jnp.where(lane < HEAD_DIM, q, zero), jnp.where(lane >= HEAD_DIM, q, zero)], axis=0)

    colf = lax.broadcasted_iota(jnp.int32, (1, blk), 1).astype(F32)
    ones = jnp.ones((blk, SLOT), BF16)

    def step(j, carry, masked):
        m_old, a_old = carry
        kb = k_ref[pl.ds(pl.multiple_of(j * blk, blk), blk), :]
        vb = v_ref[pl.ds(pl.multiple_of(j * blk, blk), blk), :]
        vaug = jnp.concatenate([vb, ones], axis=1)
        bias = (slope * LOG2E) * (colf + ((j - qi) * blk).astype(F32))
        s = _nt_dot(qs, kb) + bias
        if masked:
            t = lax.broadcasted_iota(jnp.int32, (2 * blk, blk), 0) & (blk - 1)
            s = jnp.where(lax.broadcasted_iota(jnp.int32, (2 * blk, blk), 1) <= t, s, NEG_INF)
        m_new = jnp.maximum(m_old, jnp.max(s, axis=1, keepdims=True))
        alpha = jnp.exp2(m_old - m_new)
        p = jnp.exp2(s - m_new).astype(BF16)
        return m_new, alpha * a_old + jnp.dot(p, vaug, preferred_element_type=F32)

    def finish(carry):
        _, a = carry
        a1, a2 = a[:blk], a[blk:]
        d = a1[:, :SLOT] / a1[:, SLOT:] - lam * (a2[:, :SLOT] / a2[:, SLOT:])
        d = d * lax.rsqrt(jnp.mean(d * d, axis=1, keepdims=True) + RMS_EPS)
        o_ref[...] = (d * g_ref[...] * post).astype(o_ref.dtype)

    init = (jnp.full((2 * blk, 1), NEG_INF, F32), jnp.zeros((2 * blk, 2 * SLOT), F32))
    return init, step, finish


def _sb_parts(q_ref, k_ref, v_ref, o_ref, qi, tq, tk, n_part):
    nsub = tq // tk
    lane = lax.broadcasted_iota(jnp.int32, (tq, SLOT), 1)
    q = q_ref[...]
    zero = jnp.zeros_like(q)
    qs = jnp.concatenate([jnp.where(lane < HEAD_DIM, q, zero), jnp.where(lane >= HEAD_DIM, q, zero)], axis=0)

    row = lax.broadcasted_iota(jnp.int32, (tk, tk), 0)
    col = lax.broadcasted_iota(jnp.int32, (tk, tk), 1)
    half = jnp.concatenate([jnp.where(row > col, -1.0, 0.0).astype(BF16), jnp.full((tk, tk), -1.0, BF16)], axis=1)
    cum_w = jnp.concatenate([half, half], axis=0)
    sign_bit = jnp.int32(-2 ** 31)

    sub = [slice(s * tk, (s + 1) * tk) for s in range(nsub)]
    rp = 2 * tq // n_part
    parts = [slice(p * rp, (p + 1) * rp) for p in range(n_part)]

    def chunk(c, carry, masked):
        later, acc = carry
        off = pl.multiple_of(c * tq, tq)
        kb = k_ref[pl.ds(off, tq), :]
        vb = v_ref[pl.ds(off, tq), :]
        zs = [_nt_dot(qs[pr], kb) for pr in parts]
        mid = []
        for p, z in enumerate(zs):
            neg_abs = lax.bitcast_convert_type(lax.bitcast_convert_type(z, jnp.int32) | sign_bit, F32)
            sp = jnp.maximum(z, 0.0) + LOG2E * jnp.log(1.0 + jnp.exp2(neg_abs))
            ls = z - sp
            lf = sp
            strict = None
            if masked:
                t = (lax.broadcasted_iota(jnp.int32, z.shape, 0) + p * rp) & (tq - 1)
                strict = lax.broadcasted_iota(jnp.int32, z.shape, 1) < t
                lf = jnp.where(strict, lf, 0.0)
            hi = lf.astype(BF16)
            lo = (lf - hi.astype(F32)).astype(BF16)
            stacked = jnp.concatenate([jnp.concatenate([hi[:, c_], lo[:, c_]], axis=1) for c_ in sub], axis=0)
            cs = jnp.dot(stacked, cum_w, preferred_element_type=F32)
            mid.append((ls, cs, strict))
        later_out, acc_out = [], []
        for pr, (ls, cs, strict) in zip(parts, mid):
            lat = later[pr]
            ws = [None] * nsub
            for s in reversed(range(nsub)):
                cs_s = cs[s * rp:(s + 1) * rp]
                w = jnp.exp2(ls[:, sub[s]] + cs_s[:, :tk] + lat)
                if masked:
                    w = jnp.where(strict[:, sub[s]], w, 0.0)
                ws[s] = w.astype(BF16)
                lat = lat + cs_s[:, tk:]
            later_out.append(lat)
            acc_out.append(acc[pr] + jnp.dot(jnp.concatenate(ws, axis=1), vb, preferred_element_type=F32))
        return jnp.concatenate(later_out, axis=0), jnp.concatenate(acc_out, axis=0)

    def finish(carry):
        _, acc = carry
        o_ref[...] = jnp.where(lane < HEAD_DIM, acc[:tq], acc[tq:]).astype(o_ref.dtype)

    init = (jnp.zeros((2 * tq, SLOT), F32), jnp.zeros((2 * tq, SLOT), F32))
    return init, chunk, finish


def _mixer_kernel(sc_ref, slope_ref, *refs, blk, tk, n_part, n_pair):
    hd = pl.program_id(1)
    qi = pl.program_id(2)
    ins, (od_ref, os_ref) = refs[:7 * n_pair], refs[7 * n_pair:]
    d_parts, s_parts = [], []
    for p in range(n_pair):
        dq, dk, dv, g, sq, sk, sv = ins[7 * p:7 * p + 7]
        cols = slice(p * SLOT, (p + 1) * SLOT)
        d_parts.append(_diff_parts(sc_ref, slope_ref, dq, dk, dv, g, od_ref.at[:, cols], hd * n_pair + p, qi, blk))
        s_parts.append(_sb_parts(sq, sk, sv, os_ref.at[:, cols], qi, blk, tk, n_part))

    s_carry = tuple(sp[1](qi, sp[0], True) for sp in s_parts)
    d_carry = tuple(dp[1](qi, dp[0], True) for dp in d_parts)

    def body(j, carry):
        d_carry, s_carry = carry
        return (tuple(dp[1](j, c, False) for dp, c in zip(d_parts, d_carry)),
                tuple(sp[1](qi - 1 - j, c, False) for sp, c in zip(s_parts, s_carry)))

    d_carry, s_carry = lax.fori_loop(0, qi, body, (d_carry, s_carry))
    for dp, c in zip(d_parts, d_carry):
        dp[2](c)
    for sp, c in zip(s_parts, s_carry):
        sp[2](c)


def _mixer(proj, scalars, slopes, subln_g, *, blk=512, tk=128, n_part=2, n_pair=4):
    B, S, _ = proj.shape
    kblk = ATT_WIDTH // SLOT
    base = 3 * kblk

    def qspec(off, p):
        return pl.BlockSpec((None, blk, SLOT), lambda b, h, i: (b, i, off + n_pair * h + p))

    def kvspec(off, p):
        return pl.BlockSpec((None, S, SLOT), lambda b, h, i: (b, 0, off + n_pair * h + p))

    in_specs = [pl.BlockSpec(memory_space=pltpu.SMEM), pl.BlockSpec(memory_space=pltpu.SMEM)]
    args = [scalars, slopes]
    for p in range(n_pair):
        in_specs += [qspec(0, p), kvspec(kblk, p), kvspec(2 * kblk, p),
                     pl.BlockSpec((1, SLOT), lambda b, h, i: (0, 0)),
                     qspec(base, p), kvspec(base + kblk, p), kvspec(base + 2 * kblk, p)]
        args += [proj, proj, proj, subln_g, proj, proj, proj]
    out_spec = pl.BlockSpec((None, blk, n_pair * SLOT), lambda b, h, i: (b, i, h))
    out = jax.ShapeDtypeStruct((B, S, ATT_WIDTH), BF16)
    return pl.pallas_call(
        functools.partial(_mixer_kernel, blk=blk, tk=tk, n_part=n_part, n_pair=n_pair),
        grid=(B, DIFF_SLOTS // n_pair, S // blk),
        in_specs=in_specs,
        out_specs=[out_spec, out_spec],
        out_shape=[out, out],
        compiler_params=_params(("parallel", "parallel", "parallel")),
        name="mixer",
    )(*args)


def _layer_norm(r, g, b):
    mu = jnp.mean(r, axis=-1, keepdims=True)
    c = r - mu
    var = jnp.mean(c * c, axis=-1, keepdims=True)
    return c * lax.rsqrt(var + LN_EPS) * g + b


def _out_proj_ln_kernel(h_ref, a_ref, b_ref, w_ref, g_ref, beta_ref, o_ref, ob_ref, *, alpha):
    half = a_ref.shape[1]
    y = jnp.dot(a_ref[...], w_ref[:half, :], preferred_element_type=F32)
    y = y + jnp.dot(b_ref[...], w_ref[half:, :], preferred_element_type=F32)
    out = _layer_norm(alpha * h_ref[...] + y, g_ref[...], beta_ref[...])
    o_ref[...] = out
    ob_ref[...] = out.astype(BF16)


def _out_proj_ln(h, a, b, wb, g, beta, *, alpha, tm=512):
    T, D = h.shape
    half = a.shape[1]
    return pl.pallas_call(
        functools.partial(_out_proj_ln_kernel, alpha=alpha),
        grid=(T // tm,),
        in_specs=[pl.BlockSpec((tm, D), lambda i: (i, 0)),
                  pl.BlockSpec((tm, half), lambda i: (i, 0)),
                  pl.BlockSpec((tm, half), lambda i: (i, 0)),
                  pl.BlockSpec((2 * half, D), lambda i: (0, 0)),
                  pl.BlockSpec((1, D), lambda i: (0, 0)),
                  pl.BlockSpec((1, D), lambda i: (0, 0))],
        out_specs=[pl.BlockSpec((tm, D), lambda i: (i, 0)),
                   pl.BlockSpec((tm, D), lambda i: (i, 0))],
        out_shape=[jax.ShapeDtypeStruct((T, D), F32), jax.ShapeDtypeStruct((T, D), BF16)],
        compiler_params=_params(("parallel",)),
        name="out_proj_ln",
    )(h, a, b, wb, g, beta)


def _merge_sort_network(lo, hi):
    def merge(lo, hi, r):
        step = 2 * r
        if step < hi - lo:
            yield from merge(lo, hi, step)
            yield from merge(lo + r, hi, step)
            yield from ((i, i + r) for i in range(lo + r, hi - r, step))
        else:
            yield (lo, lo + r)
    if hi - lo >= 1:
        mid = lo + (hi - lo) // 2
        yield from _merge_sort_network(lo, mid)
        yield from _merge_sort_network(mid + 1, hi)
        yield from merge(lo, hi, 1)


def _sort_network(n):
    if n & (n - 1) == 0:
        return list(_merge_sort_network(0, n - 1))
    return [(i, i + 1) for rnd in range(n) for i in range(rnd % 2, n - 1, 2)]


def _top16(s):
    n = s.shape[0] // 8
    v = [s[8 * r:8 * (r + 1), :] for r in range(n)]
    for i, j in _sort_network(n):
        v[i], v[j] = jnp.maximum(v[i], v[j]), jnp.minimum(v[i], v[j])
    vals = []
    for k in range(PEER_TOPK):
        m = jnp.max(v[0], axis=0, keepdims=True)
        vals.append(m)
        eq = v[0] == m
        for r in range(min(n, PEER_TOPK - 1 - k)):
            v[r] = jnp.where(eq, v[r + 1] if r + 1 < n else NEG_INF, v[r])
    return jnp.concatenate(vals, axis=0)


def _rank_among(s, v):
    rank = jnp.full(s.shape, float(PEER_TOPK), F32)
    for k in reversed(range(PEER_TOPK)):
        rank = jnp.where(s == v[k:k + 1, :], float(k), rank)
    return rank


def _peer_route_kernel(x_ref, wq_ref, keys_ref, rank2_ref, b_ref, cnt_ref, a_ref):
    tb = x_ref.shape[0]
    qt = _nt_dot(wq_ref[...], x_ref[...]).astype(BF16)
    row8 = lax.broadcasted_iota(jnp.int32, (8, tb), 0)
    row16 = lax.broadcasted_iota(jnp.int32, (PEER_TOPK, tb), 0)
    for h in range(PEER_HEADS):
        s1 = jnp.dot(keys_ref[2 * h], qt[(2 * h) * HALF_Q:(2 * h + 1) * HALF_Q, :], preferred_element_type=F32)
        s2 = jnp.dot(keys_ref[2 * h + 1], qt[(2 * h + 1) * HALF_Q:(2 * h + 2) * HALF_Q, :],
                     preferred_element_type=F32)
        v1 = _top16(s1)
        v2 = _top16(s2)
        rank2 = _rank_among(s2, v2)
        v2lo = v2[0:8, :]
        pieces = [v1 + v2[0:1, :],
                  v1[0:8, :] + v2[1:2, :],
                  jnp.where(row16 >= 2, v1[0:1, :] + v2, NEG_INF),
                  jnp.where(row8 >= 2, v1[1:2, :] + v2lo, NEG_INF),
                  jnp.where((row8 >= 2) & (row8 <= 4), v1[2:3, :] + v2lo, NEG_INF),
                  jnp.where((row8 >= 2) & (row8 <= 3), v1[3:4, :] + v2lo, NEG_INF),
                  jnp.where(row8 == 2, v1[4:5, :] + v2lo, NEG_INF)]
        cand = jnp.concatenate(pieces, axis=0)
        tau = _top16(cand)[PEER_TOPK - 1:PEER_TOPK, :]
        sel = cand >= tau
        top = v1[0:1, :] + v2[0:1, :]
        z = jnp.sum(jnp.where(sel, jnp.exp(cand - top), 0.0), axis=0, keepdims=True)
        self32 = jnp.where(sel, 1.0, 0.0)
        per_row = self32[0:16] + jnp.concatenate([self32[16:24], jnp.zeros((8, tb), F32)], axis=0)
        off = 24
        for k1, rows in enumerate((16, 8, 8, 8, 8)):
            per_row = per_row + jnp.where(row16 == k1, jnp.sum(self32[off:off + rows], axis=0, keepdims=True), 0.0)
            off += rows
        cnt = jnp.zeros(s1.shape, F32)
        for k1 in range(PEER_TOPK):
            cnt = jnp.where(s1 == v1[k1:k1 + 1, :], per_row[k1:k1 + 1, :], cnt)
        rank2_ref[h] = rank2.astype(BF16)
        b_ref[h] = jnp.exp(s2 - v2[0:1, :]).astype(BF16)
        cnt_ref[h] = cnt
        a_ref[h] = jnp.exp(s1 - v1[0:1, :]) * (SQRT_HALF / z)


def _peer_route(xb, wqt, keys, *, tb=512):
    T, D = xb.shape
    tab_spec = pl.BlockSpec((PEER_HEADS, N_KEYS, tb), lambda i: (0, 0, i))
    return pl.pallas_call(
        _peer_route_kernel,
        grid=(T // tb,),
        in_specs=[pl.BlockSpec((tb, D), lambda i: (i, 0)),
                  pl.BlockSpec(wqt.shape, lambda i: (0, 0)),
                  pl.BlockSpec(keys.shape, lambda i: (0, 0, 0))],
        out_specs=[tab_spec] * 4,
        out_shape=[jax.ShapeDtypeStruct((PEER_HEADS, N_KEYS, T), dt) for dt in (BF16, BF16, F32, F32)],
        compiler_params=_params(("parallel",)),
        name="peer_route",
    )(xb, wqt, keys)


def _peer_expert_kernel(h_ref, x_ref, u_ref, vt_ref, rank2_ref, b_ref, cnt_ref, a_ref, g_ref, beta_ref,
                        o_ref, ob_ref, acc_ref, hid_ref, *, alpha):
    e = pl.program_id(1)

    @pl.when(e == 0)
    def _():
        acc_ref[...] = jnp.zeros_like(acc_ref)

    tb = x_ref.shape[0]
    pack = 16
    zero = jnp.zeros((N_KEYS // pack, pack, tb), BF16)

    def row(ref, h, ii):
        return jnp.broadcast_to(ref[h, ii:ii + 1, :], (pack, tb)).astype(BF16)[None]

    act = _nt_dot(u_ref[...], x_ref[...])
    n_i = u_ref.shape[0] // N_KEYS
    per = n_i // 2
    csum = None
    for ii in range(n_i):
        gate = None
        for h in range(PEER_HEADS):
            sel = rank2_ref[h].reshape(zero.shape) < row(cnt_ref, h, ii)
            term = jnp.where(sel, b_ref[h].reshape(zero.shape), zero) * row(a_ref, h, ii)
            gate = term if gate is None else gate + term
        a = act[ii * N_KEYS:(ii + 1) * N_KEYS, :]
        hid = gate.reshape(N_KEYS, tb) * (a * (1.0 + lax.erf(a))).astype(BF16)
        hid_ref[ii * N_KEYS:(ii + 1) * N_KEYS, :] = hid
        if (ii + 1) % per == 0:
            rows = slice((ii + 1 - per) * N_KEYS, (ii + 1) * N_KEYS)
            part = jnp.dot(vt_ref[:, rows], hid_ref[rows, :], preferred_element_type=F32)
            csum = part if csum is None else csum + part
    acc_ref[...] += csum

    @pl.when(e == pl.num_programs(1) - 1)
    def _():
        y = acc_ref[...].T
        out = _layer_norm(alpha * h_ref[...] + y, g_ref[...], beta_ref[...])
        o_ref[...] = out
        ob_ref[...] = out.astype(BF16)


def _peer_expert(h, xb, ub, vtb, tables, g, beta, *, alpha, tb=512, eb=2048):
    T, D = h.shape
    E = ub.shape[0]
    rank2, bexp, cnt, a = tables
    n_i = eb // N_KEYS
    full_tab = pl.BlockSpec((PEER_HEADS, N_KEYS, tb), lambda t, e: (0, 0, t))
    row_tab = pl.BlockSpec((PEER_HEADS, n_i, tb), lambda t, e: (0, e, t))
    return pl.pallas_call(
        functools.partial(_peer_expert_kernel, alpha=alpha),
        grid=(T // tb, E // eb),
        in_specs=[pl.BlockSpec((tb, D), lambda t, e: (t, 0)),
                  pl.BlockSpec((tb, D), lambda t, e: (t, 0)),
                  pl.BlockSpec((eb, D), lambda t, e: (e, 0)),
                  pl.BlockSpec((D, eb), lambda t, e: (0, e)),
                  full_tab, full_tab, row_tab, row_tab,
                  pl.BlockSpec((1, D), lambda t, e: (0, 0)),
                  pl.BlockSpec((1, D), lambda t, e: (0, 0))],
        out_specs=[pl.BlockSpec((tb, D), lambda t, e: (t, 0)),
                   pl.BlockSpec((tb, D), lambda t, e: (t, 0))],
        out_shape=[jax.ShapeDtypeStruct((T, D), F32), jax.ShapeDtypeStruct((T, D), BF16)],
        scratch_shapes=[pltpu.VMEM((D, tb), F32), pltpu.VMEM((eb, tb), BF16)],
        compiler_params=_params(("parallel", "arbitrary")),
        name="peer_expert",
    )(h, xb, ub, vtb, rank2, bexp, cnt, a, g, beta)


def kernel(x, w_in, lam_q1, lam_k1, lam_q2, lam_k2, subln_g, w_o, ln1_g, ln1_b, w_query, sub_keys, expert_u,
           expert_v, ln2_g, ln2_b):
    B, S, D = x.shape
    T = B * S
    depth = w_in.shape[0]
    alpha = (2.0 * depth) ** 0.25
    slopes = jnp.asarray([2.0 ** (-8.0 * (i + 1) / DIFF_HEADS) for i in range(DIFF_HEADS)], F32)

    h = x.reshape(T, D)
    hb = h.astype(BF16)
    for l in range(depth):
        lambda_init = 0.8 - 0.6 * math.exp(-0.3 * l)
        lam = jnp.exp(jnp.sum(lam_q1[l] * lam_k1[l])) - jnp.exp(jnp.sum(lam_q2[l] * lam_k2[l])) + lambda_init
        scalars = jnp.stack([lam, jnp.asarray(1.0 - lambda_init, F32)]).astype(F32)

        proj = _in_proj(hb, w_in[l].astype(BF16)).reshape(B, S, -1)
        diff, sb = _mixer(proj, scalars, slopes, subln_g[l].reshape(1, SLOT))
        h, hb = _out_proj_ln(h, diff.reshape(T, ATT_WIDTH), sb.reshape(T, ATT_WIDTH), w_o[l].astype(BF16),
                             ln1_g[l].reshape(1, D), ln1_b[l].reshape(1, D), alpha=alpha)

        keys = sub_keys[l].reshape(2 * PEER_HEADS, N_KEYS, HALF_Q).astype(BF16)
        tables = _peer_route(hb, w_query[l].T.astype(BF16), keys)
        h, hb = _peer_expert(h, hb, (expert_u[l] * SQRT_HALF).astype(BF16), expert_v[l].T.astype(BF16), tables,
                             ln2_g[l].reshape(1, D), ln2_b[l].reshape(1, D), alpha=alpha)
    return h.reshape(B, S, D)
```

```python
import functools
import math

import jax
import jax.numpy as jnp
from jax import lax
from jax.experimental import pallas as pl
from jax.experimental.pallas import tpu as pltpu

F32 = jnp.float32
BF16 = jnp.bfloat16

HEAD_DIM = 64
DIFF_HEADS = 4
SB_HEADS = 8
SLOT = 2 * HEAD_DIM
DIFF_SLOTS = DIFF_HEADS
SB_SLOTS = SB_HEADS // 2
ATT_WIDTH = DIFF_SLOTS * SLOT
QUERY_GROUPS = (0, 3)
LOG2E = math.log2(math.e)
QUERY_SCALE = HEAD_DIM ** -0.5 * LOG2E
SQRT_HALF = math.sqrt(0.5)

PEER_HEADS = 8
N_KEYS = 128
PEER_TOPK = 16
HALF_Q = 128

LN_EPS = 1e-5
RMS_EPS = 1e-5

VMEM_LIMIT = 56 * 1024 * 1024
NEG_INF = float("-inf")


def _params(sem):
    return pltpu.CompilerParams(dimension_semantics=sem, vmem_limit_bytes=VMEM_LIMIT)


def _in_proj_kernel(x_ref, w_ref, o_ref, *, tn):
    x = x_ref[...]
    for n0 in range(0, o_ref.shape[1], tn):
        y = jnp.dot(x, w_ref[:, n0:n0 + tn], preferred_element_type=F32)
        if n0 // ATT_WIDTH in QUERY_GROUPS:
            y = y * QUERY_SCALE
        o_ref[:, n0:n0 + tn] = y.astype(o_ref.dtype)


def _in_proj(xb, wb, layer, *, tm=512, tn=ATT_WIDTH):
    T, K = xb.shape
    N = wb.shape[2]
    return pl.pallas_call(
        functools.partial(_in_proj_kernel, tn=tn),
        grid=(T // tm,),
        in_specs=[pl.BlockSpec((tm, K), lambda i: (i, 0)),
                  pl.BlockSpec((None, K, N), lambda i: (layer, 0, 0))],
        out_specs=pl.BlockSpec((tm, N), lambda i: (i, 0)),
        out_shape=jax.ShapeDtypeStruct((T, N), BF16),
        compiler_params=_params(("parallel",)),
        name="in_proj",
    )(xb, wb)


def _nt_dot(a, b):
    return lax.dot_general(a, b, (((1,), (1,)), ((), ())), preferred_element_type=F32)


def _diff_parts(sc_ref, slope_ref, q_ref, k_ref, v_ref, g_ref, o_ref, hd, qi, blk):
    lam = sc_ref[0]
    post = sc_ref[1]
    slope = slope_ref[hd]

    lane = lax.broadcasted_iota(jnp.int32, (blk, SLOT), 1)
    q = q_ref[...]
    zero = jnp.zeros_like(q)
    qs = jnp.concatenate([jnp.where(lane < HEAD_DIM, q, zero), jnp.where(lane >= HEAD_DIM, q, zero)], axis=0)

    colf = lax.broadcasted_iota(jnp.int32, (1, blk), 1).astype(F32)
    ones = jnp.ones((blk, SLOT), BF16)

    def step(j, carry, masked):
        m_old, a_old = carry
        kb = k_ref[pl.ds(pl.multiple_of(j * blk, blk), blk), :]
        vb = v_ref[pl.ds(pl.multiple_of(j * blk, blk), blk), :]
        vaug = jnp.concatenate([vb, ones], axis=1)
        bias = (slope * LOG2E) * (colf + ((j - qi) * blk).astype(F32))
        s = _nt_dot(qs, kb) + bias
        if masked:
            t = lax.broadcasted_iota(jnp.int32, (2 * blk, blk), 0) & (blk - 1)
            s = jnp.where(lax.broadcasted_iota(jnp.int32, (2 * blk, blk), 1) <= t, s, NEG_INF)
        m_new = jnp.maximum(m_old, jnp.max(s, axis=1, keepdims=True))
        alpha = jnp.exp2(m_old - m_new)
        p = jnp.exp2(s - m_new).astype(BF16)
        return m_new, alpha * a_old + jnp.dot(p, vaug, preferred_element_type=F32)

    def finish(carry):
        _, a = carry
        a1, a2 = a[:blk], a[blk:]
        d = a1[:, :SLOT] / a1[:, SLOT:] - lam * (a2[:, :SLOT] / a2[:, SLOT:])
        d = d * lax.rsqrt(jnp.mean(d * d, axis=1, keepdims=True) + RMS_EPS)
        o_ref[...] = (d * g_ref[...] * post).astype(o_ref.dtype)

    init = (jnp.full((2 * blk, 1), NEG_INF, F32), jnp.zeros((2 * blk, 2 * SLOT), F32))
    return init, step, finish


def _sb_parts(q_ref, k_ref, v_ref, o_ref, qi, tq, tk, n_part):
    nsub = tq // tk
    lane = lax.broadcasted_iota(jnp.int32, (tq, SLOT), 1)
    q = q_ref[...]
    zero = jnp.zeros_like(q)
    qs = jnp.concatenate([jnp.where(lane < HEAD_DIM, q, zero), jnp.where(lane >= HEAD_DIM, q, zero)], axis=0)

    row = lax.broadcasted_iota(jnp.int32, (tk, tk), 0)
    col = lax.broadcasted_iota(jnp.int32, (tk, tk), 1)
    half = jnp.concatenate([jnp.where(row > col, -1.0, 0.0).astype(BF16), jnp.full((tk, tk), -1.0, BF16)], axis=1)
    cum_w = jnp.concatenate([half, half], axis=0)
    sign_bit = jnp.int32(-2 ** 31)

    sub = [slice(s * tk, (s + 1) * tk) for s in range(nsub)]
    rp = 2 * tq // n_part
    parts = [slice(p * rp, (p + 1) * rp) for p in range(n_part)]

    def chunk(c, carry, masked):
        later, acc = carry
        off = pl.multiple_of(c * tq, tq)
        kb = k_ref[pl.ds(off, tq), :]
        vb = v_ref[pl.ds(off, tq), :]
        zs = [_nt_dot(qs[pr], kb) for pr in parts]
        mid = []
        for p, z in enumerate(zs):
            neg_abs = lax.bitcast_convert_type(lax.bitcast_convert_type(z, jnp.int32) | sign_bit, F32)
            sp = jnp.maximum(z, 0.0) + LOG2E * jnp.log(1.0 + jnp.exp2(neg_abs))
            ls = z - sp
            lf = sp
            strict = None
            if masked:
                t = (lax.broadcasted_iota(jnp.int32, z.shape, 0) + p * rp) & (tq - 1)
                strict = lax.broadcasted_iota(jnp.int32, z.shape, 1) < t
                lf = jnp.where(strict, lf, 0.0)
            hi = lf.astype(BF16)
            lo = (lf - hi.astype(F32)).astype(BF16)
            stacked = jnp.concatenate([jnp.concatenate([hi[:, c_], lo[:, c_]], axis=1) for c_ in sub], axis=0)
            cs = jnp.dot(stacked, cum_w, preferred_element_type=F32)
            mid.append((ls, cs, strict))
        later_out, acc_out = [], []
        for pr, (ls, cs, strict) in zip(parts, mid):
            lat = later[pr]
            ws = [None] * nsub
            for s in reversed(range(nsub)):
                cs_s = cs[s * rp:(s + 1) * rp]
                w = jnp.exp2(ls[:, sub[s]] + cs_s[:, :tk] + lat)
                if masked:
                    w = jnp.where(strict[:, sub[s]], w, 0.0)
                ws[s] = w.astype(BF16)
                lat = lat + cs_s[:, tk:]
            later_out.append(lat)
            acc_out.append(acc[pr] + jnp.dot(jnp.concatenate(ws, axis=1), vb, preferred_element_type=F32))
        return jnp.concatenate(later_out, axis=0), jnp.concatenate(acc_out, axis=0)

    def finish(carry):
        _, acc = carry
        o_ref[...] = jnp.where(lane < HEAD_DIM, acc[:tq], acc[tq:]).astype(o_ref.dtype)

    init = (jnp.zeros((2 * tq, SLOT), F32), jnp.zeros((2 * tq, SLOT), F32))
    return init, chunk, finish


def _mixer_kernel(sc_ref, slope_ref, *refs, blk, tk, n_part, n_pair):
    hd = pl.program_id(1)
    qi = pl.program_id(2)
    ins, (od_ref, os_ref) = refs[:7 * n_pair], refs[7 * n_pair:]
    d_parts, s_parts = [], []
    for p in range(n_pair):
        dq, dk, dv, g, sq, sk, sv = ins[7 * p:7 * p + 7]
        cols = slice(p * SLOT, (p + 1) * SLOT)
        d_parts.append(_diff_parts(sc_ref, slope_ref, dq, dk, dv, g, od_ref.at[:, cols], hd * n_pair + p, qi, blk))
        s_parts.append(_sb_parts(sq, sk, sv, os_ref.at[:, cols], qi, blk, tk, n_part))

    s_carry = tuple(sp[1](qi, sp[0], True) for sp in s_parts)
    d_carry = tuple(dp[1](qi, dp[0], True) for dp in d_parts)

    def body(j, carry):
        d_carry, s_carry = carry
        return (tuple(dp[1](j, c, False) for dp, c in zip(d_parts, d_carry)),
                tuple(sp[1](qi - 1 - j, c, False) for sp, c in zip(s_parts, s_carry)))

    d_carry, s_carry = lax.fori_loop(0, qi, body, (d_carry, s_carry))
    for dp, c in zip(d_parts, d_carry):
        dp[2](c)
    for sp, c in zip(s_parts, s_carry):
        sp[2](c)


def _mixer(proj, scalars, slopes, subln_g, *, blk=512, tk=128, n_part=2, n_pair=4):
    B, S, _ = proj.shape
    kblk = ATT_WIDTH // SLOT
    base = 3 * kblk

    def qspec(off, p):
        return pl.BlockSpec((None, blk, SLOT), lambda b, h, i: (b, i, off + n_pair * h + p))

    def kvspec(off, p):
        return pl.BlockSpec((None, S, SLOT), lambda b, h, i: (b, 0, off + n_pair * h + p))

    in_specs = [pl.BlockSpec(memory_space=pltpu.SMEM), pl.BlockSpec(memory_space=pltpu.SMEM)]
    args = [scalars, slopes]
    for p in range(n_pair):
        in_specs += [qspec(0, p), kvspec(kblk, p), kvspec(2 * kblk, p),
                     pl.BlockSpec((1, SLOT), lambda b, h, i: (0, 0)),
                     qspec(base, p), kvspec(base + kblk, p), kvspec(base + 2 * kblk, p)]
        args += [proj, proj, proj, subln_g, proj, proj, proj]
    out_spec = pl.BlockSpec((None, blk, n_pair * SLOT), lambda b, h, i: (b, i, h))
    out = jax.ShapeDtypeStruct((B, S, ATT_WIDTH), BF16)
    return pl.pallas_call(
        functools.partial(_mixer_kernel, blk=blk, tk=tk, n_part=n_part, n_pair=n_pair),
        grid=(B, DIFF_SLOTS // n_pair, S // blk),
        in_specs=in_specs,
        out_specs=[out_spec, out_spec],
        out_shape=[out, out],
        compiler_params=_params(("parallel", "parallel", "parallel")),
        name="mixer",
    )(*args)


def _layer_norm(r, g, b):
    mu = jnp.mean(r, axis=-1, keepdims=True)
    c = r - mu
    var = jnp.mean(c * c, axis=-1, keepdims=True)
    return c * lax.rsqrt(var + LN_EPS) * g + b


def _out_proj_ln_kernel(h_ref, a_ref, b_ref, w_ref, g_ref, beta_ref, o_ref, ob_ref, *, alpha):
    half = a_ref.shape[1]
    y = jnp.dot(a_ref[...], w_ref[:half, :], preferred_element_type=F32)
    y = y + jnp.dot(b_ref[...], w_ref[half:, :], preferred_element_type=F32)
    out = _layer_norm(alpha * h_ref[...] + y, g_ref[...], beta_ref[...])
    o_ref[...] = out
    ob_ref[...] = out.astype(BF16)


def _out_proj_ln(h, a, b, wb, layer, g, beta, *, alpha, tm=512):
    T, D = h.shape
    half = a.shape[1]
    return pl.pallas_call(
        functools.partial(_out_proj_ln_kernel, alpha=alpha),
        grid=(T // tm,),
        in_specs=[pl.BlockSpec((tm, D), lambda i: (i, 0)),
                  pl.BlockSpec((tm, half), lambda i: (i, 0)),
                  pl.BlockSpec((tm, half), lambda i: (i, 0)),
                  pl.BlockSpec((None, 2 * half, D), lambda i: (layer, 0, 0)),
                  pl.BlockSpec((1, D), lambda i: (0, 0)),
                  pl.BlockSpec((1, D), lambda i: (0, 0))],
        out_specs=[pl.BlockSpec((tm, D), lambda i: (i, 0)),
                   pl.BlockSpec((tm, D), lambda i: (i, 0))],
        out_shape=[jax.ShapeDtypeStruct((T, D), F32), jax.ShapeDtypeStruct((T, D), BF16)],
        compiler_params=_params(("parallel",)),
        name="out_proj_ln",
    )(h, a, b, wb, g, beta)


def _merge_sort_network(lo, hi):
    def merge(lo, hi, r):
        step = 2 * r
        if step < hi - lo:
            yield from merge(lo, hi, step)
            yield from merge(lo + r, hi, step)
            yield from ((i, i + r) for i in range(lo + r, hi - r, step))
        else:
            yield (lo, lo + r)
    if hi - lo >= 1:
        mid = lo + (hi - lo) // 2
        yield from _merge_sort_network(lo, mid)
        yield from _merge_sort_network(mid + 1, hi)
        yield from merge(lo, hi, 1)


def _sort_network(n):
    if n & (n - 1) == 0:
        return list(_merge_sort_network(0, n - 1))
    return [(i, i + 1) for rnd in range(n) for i in range(rnd % 2, n - 1, 2)]


def _top16(s):
    n = s.shape[0] // 8
    v = [s[8 * r:8 * (r + 1), :] for r in range(n)]
    for i, j in _sort_network(n):
        v[i], v[j] = jnp.maximum(v[i], v[j]), jnp.minimum(v[i], v[j])
    vals = []
    for k in range(PEER_TOPK):
        m = jnp.max(v[0], axis=0, keepdims=True)
        vals.append(m)
        eq = v[0] == m
        for r in range(min(n, PEER_TOPK - 1 - k)):
            v[r] = jnp.where(eq, v[r + 1] if r + 1 < n else NEG_INF, v[r])
    return jnp.concatenate(vals, axis=0)


def _rank_among(s, v):
    rank = jnp.full(s.shape, float(PEER_TOPK), F32)
    for k in reversed(range(PEER_TOPK)):
        rank = jnp.where(s == v[k:k + 1, :], float(k), rank)
    return rank


def _peer_route_kernel(x_ref, wq_ref, keys_ref, rank2_ref, b_ref, cnt_ref, a_ref):
    tb = x_ref.shape[0]
    qt = _nt_dot(wq_ref[...], x_ref[...]).astype(BF16)
    row8 = lax.broadcasted_iota(jnp.int32, (8, tb), 0)
    row16 = lax.broadcasted_iota(jnp.int32, (PEER_TOPK, tb), 0)
    for h in range(PEER_HEADS):
        s1 = jnp.dot(keys_ref[2 * h], qt[(2 * h) * HALF_Q:(2 * h + 1) * HALF_Q, :], preferred_element_type=F32)
        s2 = jnp.dot(keys_ref[2 * h + 1], qt[(2 * h + 1) * HALF_Q:(2 * h + 2) * HALF_Q, :],
                     preferred_element_type=F32)
        v1 = _top16(s1)
        v2 = _top16(s2)
        rank2 = _rank_among(s2, v2)
        v2lo = v2[0:8, :]
        pieces = [v1 + v2[0:1, :],
                  v1[0:8, :] + v2[1:2, :],
                  jnp.where(row16 >= 2, v1[0:1, :] + v2, NEG_INF),
                  jnp.where(row8 >= 2, v1[1:2, :] + v2lo, NEG_INF),
                  jnp.where((row8 >= 2) & (row8 <= 4), v1[2:3, :] + v2lo, NEG_INF),
                  jnp.where((row8 >= 2) & (row8 <= 3), v1[3:4, :] + v2lo, NEG_INF),
                  jnp.where(row8 == 2, v1[4:5, :] + v2lo, NEG_INF)]
        cand = jnp.concatenate(pieces, axis=0)
        tau = _top16(cand)[PEER_TOPK - 1:PEER_TOPK, :]
        sel = cand >= tau
        top = v1[0:1, :] + v2[0:1, :]
        z = jnp.sum(jnp.where(sel, jnp.exp(cand - top), 0.0), axis=0, keepdims=True)
        self32 = jnp.where(sel, 1.0, 0.0)
        per_row = self32[0:16] + jnp.concatenate([self32[16:24], jnp.zeros((8, tb), F32)], axis=0)
        off = 24
        for k1, rows in enumerate((16, 8, 8, 8, 8)):
            per_row = per_row + jnp.where(row16 == k1, jnp.sum(self32[off:off + rows], axis=0, keepdims=True), 0.0)
            off += rows
        cnt = jnp.zeros(s1.shape, F32)
        for k1 in range(PEER_TOPK):
            cnt = jnp.where(s1 == v1[k1:k1 + 1, :], per_row[k1:k1 + 1, :], cnt)
        rank2_ref[h] = rank2.astype(BF16)
        b_ref[h] = jnp.exp(s2 - v2[0:1, :]).astype(BF16)
        cnt_ref[h] = cnt
        a_ref[h] = jnp.exp(s1 - v1[0:1, :]) * (SQRT_HALF / z)


def _peer_route(xb, wqt, keys, layer, *, tb=512):
    T, D = xb.shape
    tab_spec = pl.BlockSpec((PEER_HEADS, N_KEYS, tb), lambda i: (0, 0, i))
    return pl.pallas_call(
        _peer_route_kernel,
        grid=(T // tb,),
        in_specs=[pl.BlockSpec((tb, D), lambda i: (i, 0)),
                  pl.BlockSpec((None,) + wqt.shape[1:], lambda i: (layer, 0, 0)),
                  pl.BlockSpec((None,) + keys.shape[1:], lambda i: (layer, 0, 0, 0))],
        out_specs=[tab_spec] * 4,
        out_shape=[jax.ShapeDtypeStruct((PEER_HEADS, N_KEYS, T), dt) for dt in (BF16, BF16, F32, F32)],
        compiler_params=_params(("parallel",)),
        name="peer_route",
    )(xb, wqt, keys)


def _peer_expert_kernel(h_ref, x_ref, u_ref, vt_ref, rank2_ref, b_ref, cnt_ref, a_ref, g_ref, beta_ref,
                        o_ref, ob_ref, acc_ref, hid_ref, *, alpha):
    e = pl.program_id(1)

    @pl.when(e == 0)
    def _():
        acc_ref[...] = jnp.zeros_like(acc_ref)

    tb = x_ref.shape[0]
    pack = 16
    zero = jnp.zeros((N_KEYS // pack, pack, tb), BF16)

    def row(ref, h, ii):
        return jnp.broadcast_to(ref[h, ii:ii + 1, :], (pack, tb)).astype(BF16)[None]

    act = _nt_dot(u_ref[...], x_ref[...])
    n_i = u_ref.shape[0] // N_KEYS
    per = n_i // 2
    csum = None
    for ii in range(n_i):
        gate = None
        for h in range(PEER_HEADS):
            sel = rank2_ref[h].reshape(zero.shape) < row(cnt_ref, h, ii)
            term = jnp.where(sel, b_ref[h].reshape(zero.shape), zero) * row(a_ref, h, ii)
            gate = term if gate is None else gate + term
        a = act[ii * N_KEYS:(ii + 1) * N_KEYS, :]
        hid = gate.reshape(N_KEYS, tb) * (a * (1.0 + lax.erf(a))).astype(BF16)
        hid_ref[ii * N_KEYS:(ii + 1) * N_KEYS, :] = hid
        if (ii + 1) % per == 0:
            rows = slice((ii + 1 - per) * N_KEYS, (ii + 1) * N_KEYS)
            part = jnp.dot(vt_ref[:, rows], hid_ref[rows, :], preferred_element_type=F32)
            csum = part if csum is None else csum + part
    acc_ref[...] += csum

    @pl.when(e == pl.num_programs(1) - 1)
    def _():
        y = acc_ref[...].T
        out = _layer_norm(alpha * h_ref[...] + y, g_ref[...], beta_ref[...])
        o_ref[...] = out
        ob_ref[...] = out.astype(BF16)


def _peer_expert(h, xb, ub, vtb, layer, tables, g, beta, *, alpha, tb=512, eb=2048):
    T, D = h.shape
    E = ub.shape[1]
    rank2, bexp, cnt, a = tables
    n_i = eb // N_KEYS
    full_tab = pl.BlockSpec((PEER_HEADS, N_KEYS, tb), lambda t, e: (0, 0, t))
    row_tab = pl.BlockSpec((PEER_HEADS, n_i, tb), lambda t, e: (0, e, t))
    return pl.pallas_call(
        functools.partial(_peer_expert_kernel, alpha=alpha),
        grid=(T // tb, E // eb),
        in_specs=[pl.BlockSpec((tb, D), lambda t, e: (t, 0)),
                  pl.BlockSpec((tb, D), lambda t, e: (t, 0)),
                  pl.BlockSpec((None, eb, D), lambda t, e: (layer, e, 0)),
                  pl.BlockSpec((None, D, eb), lambda t, e: (layer, 0, e)),
                  full_tab, full_tab, row_tab, row_tab,
                  pl.BlockSpec((1, D), lambda t, e: (0, 0)),
                  pl.BlockSpec((1, D), lambda t, e: (0, 0))],
        out_specs=[pl.BlockSpec((tb, D), lambda t, e: (t, 0)),
                   pl.BlockSpec((tb, D), lambda t, e: (t, 0))],
        out_shape=[jax.ShapeDtypeStruct((T, D), F32), jax.ShapeDtypeStruct((T, D), BF16)],
        scratch_shapes=[pltpu.VMEM((D, tb), F32), pltpu.VMEM((eb, tb), BF16)],
        compiler_params=_params(("parallel", "arbitrary")),
        name="peer_expert",
    )(h, xb, ub, vtb, rank2, bexp, cnt, a, g, beta)


def kernel(x, w_in, lam_q1, lam_k1, lam_q2, lam_k2, subln_g, w_o, ln1_g, ln1_b, w_query, sub_keys, expert_u,
           expert_v, ln2_g, ln2_b):
    B, S, D = x.shape
    T = B * S
    depth = w_in.shape[0]
    alpha = (2.0 * depth) ** 0.25
    slopes = jnp.asarray([2.0 ** (-8.0 * (i + 1) / DIFF_HEADS) for i in range(DIFF_HEADS)], F32)

    w_in_b = w_in.astype(BF16)
    w_o_b = w_o.astype(BF16)
    wq_t = jnp.swapaxes(w_query, 1, 2).astype(BF16)
    keys = sub_keys.reshape(depth, 2 * PEER_HEADS, N_KEYS, HALF_Q).astype(BF16)
    u_b = (expert_u * SQRT_HALF).astype(BF16)
    vt_b = jnp.swapaxes(expert_v, 1, 2).astype(BF16)

    h = x.reshape(T, D)
    hb = h.astype(BF16)
    for l in range(depth):
        lambda_init = 0.8 - 0.6 * math.exp(-0.3 * l)
        lam = jnp.exp(jnp.sum(lam_q1[l] * lam_k1[l])) - jnp.exp(jnp.sum(lam_q2[l] * lam_k2[l])) + lambda_init
        scalars = jnp.stack([lam, jnp.asarray(1.0 - lambda_init, F32)]).astype(F32)

        proj = _in_proj(hb, w_in_b, l).reshape(B, S, -1)
        diff, sb = _mixer(proj, scalars, slopes, subln_g[l].reshape(1, SLOT))
        h, hb = _out_proj_ln(h, diff.reshape(T, ATT_WIDTH), sb.reshape(T, ATT_WIDTH), w_o_b, l,
                             ln1_g[l].reshape(1, D), ln1_b[l].reshape(1, D), alpha=alpha)

        tables = _peer_route(hb, wq_t, keys, l)
        h, hb = _peer_expert(h, hb, u_b, vt_b, l, tables,
                             ln2_g[l].reshape(1, D), ln2_b[l].reshape(1, D), alpha=alpha)
    return h.reshape(B, S, D)
```

```python
import functools
import math

import jax
import jax.numpy as jnp
from jax import lax
from jax.experimental import pallas as pl
from jax.experimental.pallas import tpu as pltpu

F32 = jnp.float32
BF16 = jnp.bfloat16

HEAD_DIM = 64
DIFF_HEADS = 4
SB_HEADS = 8
SLOT = 2 * HEAD_DIM
DIFF_SLOTS = DIFF_HEADS
ATT_WIDTH = DIFF_SLOTS * SLOT
QUERY_GROUPS = (0, 3)
LOG2E = math.log2(math.e)
QUERY_SCALE = HEAD_DIM ** -0.5 * LOG2E
SQRT_HALF = math.sqrt(0.5)

PEER_HEADS = 8
N_KEYS = 128
PEER_TOPK = 16
HALF_Q = 128

LN_EPS = 1e-5
RMS_EPS = 1e-5

V7X_VMEM_BYTES = 64 * 1024 * 1024
VMEM_LIMIT = V7X_VMEM_BYTES * 7 // 8
NEG_INF = float("-inf")


def _params(sem):
    return pltpu.CompilerParams(dimension_semantics=sem, vmem_limit_bytes=VMEM_LIMIT)


def _in_proj_kernel(x_ref, w_ref, o_ref, *, tn):
    x = x_ref[...]
    for n0 in range(0, o_ref.shape[1], tn):
        y = jnp.dot(x, w_ref[:, n0:n0 + tn], preferred_element_type=F32)
        if n0 // ATT_WIDTH in QUERY_GROUPS:
            y = y * QUERY_SCALE
        o_ref[:, n0:n0 + tn] = y.astype(o_ref.dtype)


def _in_proj(xb, wb, layer, *, tm=512, tn=ATT_WIDTH):
    T, K = xb.shape
    N = wb.shape[2]
    return pl.pallas_call(
        functools.partial(_in_proj_kernel, tn=tn),
        grid=(T // tm,),
        in_specs=[pl.BlockSpec((tm, K), lambda i: (i, 0)),
                  pl.BlockSpec((None, K, N), lambda i: (layer, 0, 0))],
        out_specs=pl.BlockSpec((tm, N), lambda i: (i, 0)),
        out_shape=jax.ShapeDtypeStruct((T, N), BF16),
        compiler_params=_params(("parallel",)),
        name="in_proj",
    )(xb, wb)


def _nt_dot(a, b):
    return lax.dot_general(a, b, (((1,), (1,)), ((), ())), preferred_element_type=F32)


def _diff_parts(sc_ref, slope_ref, q_ref, k_ref, v_ref, g_ref, o_ref, hd, qi, blk):
    lam = sc_ref[0]
    post = sc_ref[1]
    slope = slope_ref[hd]

    lane = lax.broadcasted_iota(jnp.int32, (blk, SLOT), 1)
    q = q_ref[...]
    zero = jnp.zeros_like(q)
    qs = jnp.concatenate([jnp.where(lane < HEAD_DIM, q, zero), jnp.where(lane >= HEAD_DIM, q, zero)], axis=0)

    colf = lax.broadcasted_iota(jnp.int32, (1, blk), 1).astype(F32)
    ones = jnp.ones((blk, SLOT), BF16)

    def step(j, carry, masked):
        m_old, a_old = carry
        kb = k_ref[pl.ds(pl.multiple_of(j * blk, blk), blk), :]
        vb = v_ref[pl.ds(pl.multiple_of(j * blk, blk), blk), :]
        vaug = jnp.concatenate([vb, ones], axis=1)
        bias = (slope * LOG2E) * (colf + ((j - qi) * blk).astype(F32))
        s = _nt_dot(qs, kb) + bias
        if masked:
            t = lax.broadcasted_iota(jnp.int32, (2 * blk, blk), 0) & (blk - 1)
            s = jnp.where(lax.broadcasted_iota(jnp.int32, (2 * blk, blk), 1) <= t, s, NEG_INF)
        m_new = jnp.maximum(m_old, jnp.max(s, axis=1, keepdims=True))
        alpha = jnp.exp2(m_old - m_new)
        p = jnp.exp2(s - m_new).astype(BF16)
        return m_new, alpha * a_old + jnp.dot(p, vaug, preferred_element_type=F32)

    def finish(carry):
        _, a = carry
        a1, a2 = a[:blk], a[blk:]
        d = a1[:, :SLOT] / a1[:, SLOT:] - lam * (a2[:, :SLOT] / a2[:, SLOT:])
        d = d * lax.rsqrt(jnp.mean(d * d, axis=1, keepdims=True) + RMS_EPS)
        o_ref[...] = (d * g_ref[...] * post).astype(o_ref.dtype)

    init = (jnp.full((2 * blk, 1), NEG_INF, F32), jnp.zeros((2 * blk, 2 * SLOT), F32))
    return init, step, finish


def _sb_parts(q_ref, k_ref, v_ref, o_ref, qi, tq, tk, n_part):
    nsub = tq // tk
    lane = lax.broadcasted_iota(jnp.int32, (tq, SLOT), 1)
    q = q_ref[...]
    zero = jnp.zeros_like(q)
    qs = jnp.concatenate([jnp.where(lane < HEAD_DIM, q, zero), jnp.where(lane >= HEAD_DIM, q, zero)], axis=0)

    row = lax.broadcasted_iota(jnp.int32, (tk, tk), 0)
    col = lax.broadcasted_iota(jnp.int32, (tk, tk), 1)
    half = jnp.concatenate([jnp.where(row > col, -1.0, 0.0).astype(BF16), jnp.full((tk, tk), -1.0, BF16)], axis=1)
    cum_w = jnp.concatenate([half, half], axis=0)
    sign_bit = jnp.int32(-2 ** 31)

    sub = [slice(s * tk, (s + 1) * tk) for s in range(nsub)]
    rp = 2 * tq // n_part
    parts = [slice(p * rp, (p + 1) * rp) for p in range(n_part)]

    def chunk(c, carry, masked):
        later, acc = carry
        off = pl.multiple_of(c * tq, tq)
        kb = k_ref[pl.ds(off, tq), :]
        vb = v_ref[pl.ds(off, tq), :]
        zs = [_nt_dot(qs[pr], kb) for pr in parts]
        mid = []
        for p, z in enumerate(zs):
            neg_abs = lax.bitcast_convert_type(lax.bitcast_convert_type(z, jnp.int32) | sign_bit, F32)
            sp = jnp.maximum(z, 0.0) + LOG2E * jnp.log(1.0 + jnp.exp2(neg_abs))
            ls = z - sp
            lf = sp
            strict = None
            if masked:
                t = (lax.broadcasted_iota(jnp.int32, z.shape, 0) + p * rp) & (tq - 1)
                strict = lax.broadcasted_iota(jnp.int32, z.shape, 1) < t
                lf = jnp.where(strict, lf, 0.0)
            hi = lf.astype(BF16)
            lo = (lf - hi.astype(F32)).astype(BF16)
            stacked = jnp.concatenate([jnp.concatenate([hi[:, c_], lo[:, c_]], axis=1) for c_ in sub], axis=0)
            cs = jnp.dot(stacked, cum_w, preferred_element_type=F32)
            mid.append((ls, cs, strict))
        later_out, acc_out = [], []
        for pr, (ls, cs, strict) in zip(parts, mid):
            lat = later[pr]
            ws = [None] * nsub
            for s in reversed(range(nsub)):
                cs_s = cs[s * rp:(s + 1) * rp]
                w = jnp.exp2(ls[:, sub[s]] + cs_s[:, :tk] + lat)
                if masked:
                    w = jnp.where(strict[:, sub[s]], w, 0.0)
                ws[s] = w.astype(BF16)
                lat = lat + cs_s[:, tk:]
            later_out.append(lat)
            acc_out.append(acc[pr] + jnp.dot(jnp.concatenate(ws, axis=1), vb, preferred_element_type=F32))
        return jnp.concatenate(later_out, axis=0), jnp.concatenate(acc_out, axis=0)

    def finish(carry):
        _, acc = carry
        o_ref[...] = jnp.where(lane < HEAD_DIM, acc[:tq], acc[tq:]).astype(o_ref.dtype)

    init = (jnp.zeros((2 * tq, SLOT), F32), jnp.zeros((2 * tq, SLOT), F32))
    return init, chunk, finish


def _mixer_kernel(sc_ref, slope_ref, *refs, blk, tk, n_part, n_pair):
    hd = pl.program_id(1)
    qi = pl.program_id(2)
    ins, (od_ref, os_ref) = refs[:7 * n_pair], refs[7 * n_pair:]
    d_parts, s_parts = [], []
    for p in range(n_pair):
        dq, dk, dv, g, sq, sk, sv = ins[7 * p:7 * p + 7]
        cols = slice(p * SLOT, (p + 1) * SLOT)
        d_parts.append(_diff_parts(sc_ref, slope_ref, dq, dk, dv, g, od_ref.at[:, cols], hd * n_pair + p, qi, blk))
        s_parts.append(_sb_parts(sq, sk, sv, os_ref.at[:, cols], qi, blk, tk, n_part))

    s_carry = tuple(sp[1](qi, sp[0], True) for sp in s_parts)
    d_carry = tuple(dp[1](qi, dp[0], True) for dp in d_parts)

    def body(j, carry):
        d_carry, s_carry = carry
        return (tuple(dp[1](j, c, False) for dp, c in zip(d_parts, d_carry)),
                tuple(sp[1](qi - 1 - j, c, False) for sp, c in zip(s_parts, s_carry)))

    d_carry, s_carry = lax.fori_loop(0, qi, body, (d_carry, s_carry))
    for dp, c in zip(d_parts, d_carry):
        dp[2](c)
    for sp, c in zip(s_parts, s_carry):
        sp[2](c)


def _mixer(proj, scalars, slopes, subln_g, *, blk=512, tk=128, n_part=2, n_pair=4):
    B, S, _ = proj.shape
    kblk = ATT_WIDTH // SLOT
    base = 3 * kblk

    def qspec(off, p):
        return pl.BlockSpec((None, blk, SLOT), lambda b, h, i: (b, i, off + n_pair * h + p))

    def kvspec(off, p):
        return pl.BlockSpec((None, S, SLOT), lambda b, h, i: (b, 0, off + n_pair * h + p))

    in_specs = [pl.BlockSpec(memory_space=pltpu.SMEM), pl.BlockSpec(memory_space=pltpu.SMEM)]
    args = [scalars, slopes]
    for p in range(n_pair):
        in_specs += [qspec(0, p), kvspec(kblk, p), kvspec(2 * kblk, p),
                     pl.BlockSpec((1, SLOT), lambda b, h, i: (0, 0)),
                     qspec(base, p), kvspec(base + kblk, p), kvspec(base + 2 * kblk, p)]
        args += [proj, proj, proj, subln_g, proj, proj, proj]
    out_spec = pl.BlockSpec((None, blk, n_pair * SLOT), lambda b, h, i: (b, i, h))
    out = jax.ShapeDtypeStruct((B, S, ATT_WIDTH), BF16)
    return pl.pallas_call(
        functools.partial(_mixer_kernel, blk=blk, tk=tk, n_part=n_part, n_pair=n_pair),
        grid=(B, DIFF_SLOTS // n_pair, S // blk),
        in_specs=in_specs,
        out_specs=[out_spec, out_spec],
        out_shape=[out, out],
        compiler_params=_params(("parallel", "parallel", "parallel")),
        name="mixer",
    )(*args)


def _layer_norm(r, g, b):
    mu = jnp.mean(r, axis=-1, keepdims=True)
    c = r - mu
    var = jnp.mean(c * c, axis=-1, keepdims=True)
    return c * lax.rsqrt(var + LN_EPS) * g + b


def _out_proj_ln_kernel(h_ref, a_ref, b_ref, w_ref, g_ref, beta_ref, o_ref, ob_ref, *, alpha):
    half = a_ref.shape[1]
    y = jnp.dot(a_ref[...], w_ref[:half, :], preferred_element_type=F32)
    y = y + jnp.dot(b_ref[...], w_ref[half:, :], preferred_element_type=F32)
    out = _layer_norm(alpha * h_ref[...] + y, g_ref[...], beta_ref[...])
    o_ref[...] = out
    ob_ref[...] = out.astype(BF16)


def _out_proj_ln(h, a, b, wb, layer, g, beta, *, alpha, tm=512):
    T, D = h.shape
    half = a.shape[1]
    return pl.pallas_call(
        functools.partial(_out_proj_ln_kernel, alpha=alpha),
        grid=(T // tm,),
        in_specs=[pl.BlockSpec((tm, D), lambda i: (i, 0)),
                  pl.BlockSpec((tm, half), lambda i: (i, 0)),
                  pl.BlockSpec((tm, half), lambda i: (i, 0)),
                  pl.BlockSpec((None, 2 * half, D), lambda i: (layer, 0, 0)),
                  pl.BlockSpec((1, D), lambda i: (0, 0)),
                  pl.BlockSpec((1, D), lambda i: (0, 0))],
        out_specs=[pl.BlockSpec((tm, D), lambda i: (i, 0)),
                   pl.BlockSpec((tm, D), lambda i: (i, 0))],
        out_shape=[jax.ShapeDtypeStruct((T, D), F32), jax.ShapeDtypeStruct((T, D), BF16)],
        compiler_params=_params(("parallel",)),
        name="out_proj_ln",
    )(h, a, b, wb, g, beta)


def _merge_sort_network(lo, hi):
    def merge(lo, hi, r):
        step = 2 * r
        if step < hi - lo:
            yield from merge(lo, hi, step)
            yield from merge(lo + r, hi, step)
            yield from ((i, i + r) for i in range(lo + r, hi - r, step))
        else:
            yield (lo, lo + r)
    if hi - lo >= 1:
        mid = lo + (hi - lo) // 2
        yield from _merge_sort_network(lo, mid)
        yield from _merge_sort_network(mid + 1, hi)
        yield from merge(lo, hi, 1)


def _sort_network(n):
    if n & (n - 1) == 0:
        return list(_merge_sort_network(0, n - 1))
    return [(i, i + 1) for rnd in range(n) for i in range(rnd % 2, n - 1, 2)]


def _top16(s):
    n = s.shape[0] // 8
    v = [s[8 * r:8 * (r + 1), :] for r in range(n)]
    for i, j in _sort_network(n):
        v[i], v[j] = jnp.maximum(v[i], v[j]), jnp.minimum(v[i], v[j])
    vals = []
    for k in range(PEER_TOPK):
        m = jnp.max(v[0], axis=0, keepdims=True)
        vals.append(m)
        eq = v[0] == m
        for r in range(min(n, PEER_TOPK - 1 - k)):
            v[r] = jnp.where(eq, v[r + 1] if r + 1 < n else NEG_INF, v[r])
    return jnp.concatenate(vals, axis=0)


def _rank_among(s, v):
    rank = jnp.full(s.shape, float(PEER_TOPK), F32)
    for k in reversed(range(PEER_TOPK)):
        rank = jnp.where(s == v[k:k + 1, :], float(k), rank)
    return rank


def _peer_route_kernel(x_ref, wq_ref, keys_ref, rank2_ref, b_ref, cnt_ref, a_ref):
    tb = x_ref.shape[0]
    qt = _nt_dot(wq_ref[...], x_ref[...]).astype(BF16)
    row8 = lax.broadcasted_iota(jnp.int32, (8, tb), 0)
    row16 = lax.broadcasted_iota(jnp.int32, (PEER_TOPK, tb), 0)
    for h in range(PEER_HEADS):
        s1 = jnp.dot(keys_ref[2 * h], qt[(2 * h) * HALF_Q:(2 * h + 1) * HALF_Q, :], preferred_element_type=F32)
        s2 = jnp.dot(keys_ref[2 * h + 1], qt[(2 * h + 1) * HALF_Q:(2 * h + 2) * HALF_Q, :],
                     preferred_element_type=F32)
        v1 = _top16(s1)
        v2 = _top16(s2)
        rank2 = _rank_among(s2, v2)
        v2lo = v2[0:8, :]
        pieces = [v1 + v2[0:1, :],
                  v1[0:8, :] + v2[1:2, :],
                  jnp.where(row16 >= 2, v1[0:1, :] + v2, NEG_INF),
                  jnp.where(row8 >= 2, v1[1:2, :] + v2lo, NEG_INF),
                  jnp.where((row8 >= 2) & (row8 <= 4), v1[2:3, :] + v2lo, NEG_INF),
                  jnp.where((row8 >= 2) & (row8 <= 3), v1[3:4, :] + v2lo, NEG_INF),
                  jnp.where(row8 == 2, v1[4:5, :] + v2lo, NEG_INF)]
        cand = jnp.concatenate(pieces, axis=0)
        tau = _top16(cand)[PEER_TOPK - 1:PEER_TOPK, :]
        sel = cand >= tau
        top = v1[0:1, :] + v2[0:1, :]
        z = jnp.sum(jnp.where(sel, jnp.exp(cand - top), 0.0), axis=0, keepdims=True)
        self32 = jnp.where(sel, 1.0, 0.0)
        per_row = self32[0:16] + jnp.concatenate([self32[16:24], jnp.zeros((8, tb), F32)], axis=0)
        off = 24
        for k1, rows in enumerate((16, 8, 8, 8, 8)):
            per_row = per_row + jnp.where(row16 == k1, jnp.sum(self32[off:off + rows], axis=0, keepdims=True), 0.0)
            off += rows
        cnt = jnp.zeros(s1.shape, F32)
        for k1 in range(PEER_TOPK):
            cnt = jnp.where(s1 == v1[k1:k1 + 1, :], per_row[k1:k1 + 1, :], cnt)
        rank2_ref[h] = rank2.astype(BF16)
        b_ref[h] = jnp.exp(s2 - v2[0:1, :]).astype(BF16)
        cnt_ref[h] = cnt
        a_ref[h] = jnp.exp(s1 - v1[0:1, :]) * (SQRT_HALF / z)


def _peer_route(xb, wqt, keys, layer, *, tb=512):
    T, D = xb.shape
    tab_spec = pl.BlockSpec((PEER_HEADS, N_KEYS, tb), lambda i: (0, 0, i))
    return pl.pallas_call(
        _peer_route_kernel,
        grid=(T // tb,),
        in_specs=[pl.BlockSpec((tb, D), lambda i: (i, 0)),
                  pl.BlockSpec((None,) + wqt.shape[1:], lambda i: (layer, 0, 0)),
                  pl.BlockSpec((None,) + keys.shape[1:], lambda i: (layer, 0, 0, 0))],
        out_specs=[tab_spec] * 4,
        out_shape=[jax.ShapeDtypeStruct((PEER_HEADS, N_KEYS, T), dt) for dt in (BF16, BF16, F32, F32)],
        compiler_params=_params(("parallel",)),
        name="peer_route",
    )(xb, wqt, keys)


def _peer_expert_kernel(h_ref, x_ref, u_ref, vt_ref, rank2_ref, b_ref, cnt_ref, a_ref, g_ref, beta_ref,
                        o_ref, ob_ref, acc_ref, hid_ref, *, alpha):
    e = pl.program_id(1)

    @pl.when(e == 0)
    def _():
        acc_ref[...] = jnp.zeros_like(acc_ref)

    tb = x_ref.shape[0]
    pack = 16
    zero = jnp.zeros((N_KEYS // pack, pack, tb), BF16)

    def row(ref, h, ii):
        return jnp.broadcast_to(ref[h, ii:ii + 1, :], (pack, tb)).astype(BF16)[None]

    act = _nt_dot(u_ref[...], x_ref[...])
    n_i = u_ref.shape[0] // N_KEYS
    per = n_i // 2
    csum = None
    for ii in range(n_i):
        gate = None
        for h in range(PEER_HEADS):
            sel = rank2_ref[h].reshape(zero.shape) < row(cnt_ref, h, ii)
            term = jnp.where(sel, b_ref[h].reshape(zero.shape), zero) * row(a_ref, h, ii)
            gate = term if gate is None else gate + term
        a = act[ii * N_KEYS:(ii + 1) * N_KEYS, :]
        hid = gate.reshape(N_KEYS, tb) * (a * (1.0 + lax.erf(a))).astype(BF16)
        hid_ref[ii * N_KEYS:(ii + 1) * N_KEYS, :] = hid
        if (ii + 1) % per == 0:
            rows = slice((ii + 1 - per) * N_KEYS, (ii + 1) * N_KEYS)
            part = jnp.dot(vt_ref[:, rows], hid_ref[rows, :], preferred_element_type=F32)
            csum = part if csum is None else csum + part
    acc_ref[...] += csum

    @pl.when(e == pl.num_programs(1) - 1)
    def _():
        y = acc_ref[...].T
        out = _layer_norm(alpha * h_ref[...] + y, g_ref[...], beta_ref[...])
        o_ref[...] = out
        ob_ref[...] = out.astype(BF16)


def _peer_expert(h, xb, ub, vtb, layer, tables, g, beta, *, alpha, tb=512, eb=2048):
    T, D = h.shape
    E = ub.shape[1]
    rank2, bexp, cnt, a = tables
    n_i = eb // N_KEYS
    full_tab = pl.BlockSpec((PEER_HEADS, N_KEYS, tb), lambda t, e: (0, 0, t))
    row_tab = pl.BlockSpec((PEER_HEADS, n_i, tb), lambda t, e: (0, e, t))
    return pl.pallas_call(
        functools.partial(_peer_expert_kernel, alpha=alpha),
        grid=(T // tb, E // eb),
        in_specs=[pl.BlockSpec((tb, D), lambda t, e: (t, 0)),
                  pl.BlockSpec((tb, D), lambda t, e: (t, 0)),
                  pl.BlockSpec((None, eb, D), lambda t, e: (layer, e, 0)),
                  pl.BlockSpec((None, D, eb), lambda t, e: (layer, 0, e)),
                  full_tab, full_tab, row_tab, row_tab,
                  pl.BlockSpec((1, D), lambda t, e: (0, 0)),
                  pl.BlockSpec((1, D), lambda t, e: (0, 0))],
        out_specs=[pl.BlockSpec((tb, D), lambda t, e: (t, 0)),
                   pl.BlockSpec((tb, D), lambda t, e: (t, 0))],
        out_shape=[jax.ShapeDtypeStruct((T, D), F32), jax.ShapeDtypeStruct((T, D), BF16)],
        scratch_shapes=[pltpu.VMEM((D, tb), F32), pltpu.VMEM((eb, tb), BF16)],
        compiler_params=_params(("parallel", "arbitrary")),
        name="peer_expert",
    )(h, xb, ub, vtb, rank2, bexp, cnt, a, g, beta)


def kernel(x, w_in, lam_q1, lam_k1, lam_q2, lam_k2, subln_g, w_o, ln1_g, ln1_b, w_query, sub_keys, expert_u,
           expert_v, ln2_g, ln2_b):
    B, S, D = x.shape
    T = B * S
    depth = w_in.shape[0]
    alpha = (2.0 * depth) ** 0.25
    slopes = jnp.asarray([2.0 ** (-8.0 * (i + 1) / DIFF_HEADS) for i in range(DIFF_HEADS)], F32)

    w_in_b = w_in.astype(BF16)
    w_o_b = w_o.astype(BF16)
    wq_t = jnp.swapaxes(w_query, 1, 2).astype(BF16)
    keys = sub_keys.reshape(depth, 2 * PEER_HEADS, N_KEYS, HALF_Q).astype(BF16)
    u_b = (expert_u * SQRT_HALF).astype(BF16)
    vt_b = jnp.swapaxes(expert_v, 1, 2).astype(BF16)

    h = x.reshape(T, D)
    hb = h.astype(BF16)
    for l in range(depth):
        lambda_init = 0.8 - 0.6 * math.exp(-0.3 * l)
        lam = jnp.exp(jnp.sum(lam_q1[l] * lam_k1[l])) - jnp.exp(jnp.sum(lam_q2[l] * lam_k2[l])) + lambda_init
        scalars = jnp.stack([lam, jnp.asarray(1.0 - lambda_init, F32)]).astype(F32)

        proj = _in_proj(hb, w_in_b, l).reshape(B, S, -1)
        diff, sb = _mixer(proj, scalars, slopes, subln_g[l].reshape(1, SLOT))
        h, hb = _out_proj_ln(h, diff.reshape(T, ATT_WIDTH), sb.reshape(T, ATT_WIDTH), w_o_b, l,
                             ln1_g[l].reshape(1, D), ln1_b[l].reshape(1, D), alpha=alpha)

        tables = _peer_route(hb, wq_t, keys, l)
        h, hb = _peer_expert(h, hb, u_b, vt_b, l, tables,
                             ln2_g[l].reshape(1, D), ln2_b[l].reshape(1, D), alpha=alpha)
    return h.reshape(B, S, D)
```

```python
import functools
import math

import jax
import jax.numpy as jnp
from jax import lax
from jax.experimental import pallas as pl
from jax.experimental.pallas import tpu as pltpu

F32 = jnp.float32
BF16 = jnp.bfloat16

HEAD_DIM = 64
DIFF_HEADS = 4
SB_HEADS = 8
SLOT = 2 * HEAD_DIM
DIFF_SLOTS = DIFF_HEADS
ATT_WIDTH = DIFF_SLOTS * SLOT
QUERY_GROUPS = (0, 3)
LOG2E = math.log2(math.e)
QUERY_SCALE = HEAD_DIM ** -0.5 * LOG2E
SQRT_HALF = math.sqrt(0.5)

PEER_HEADS = 8
N_KEYS = 128
PEER_TOPK = 16
HALF_Q = 128

LN_EPS = 1e-5
RMS_EPS = 1e-5

LANES = 128
V7X_VMEM_BYTES = 64 * 1024 * 1024
VMEM_LIMIT = V7X_VMEM_BYTES * 7 // 8
NEG_INF = float("-inf")


def _params(sem):
    return pltpu.CompilerParams(dimension_semantics=sem, vmem_limit_bytes=VMEM_LIMIT)


def _in_proj_kernel(x_ref, w_ref, o_ref, *, tn):
    x = x_ref[...]
    for n0 in range(0, o_ref.shape[1], tn):
        y = jnp.dot(x, w_ref[:, n0:n0 + tn], preferred_element_type=F32)
        if n0 // ATT_WIDTH in QUERY_GROUPS:
            y = y * QUERY_SCALE
        o_ref[:, n0:n0 + tn] = y.astype(o_ref.dtype)


def _in_proj(xb, wb, layer, *, tm=512, tn=ATT_WIDTH):
    T, K = xb.shape
    N = wb.shape[2]
    return pl.pallas_call(
        functools.partial(_in_proj_kernel, tn=tn),
        grid=(T // tm,),
        in_specs=[pl.BlockSpec((tm, K), lambda i: (i, 0)),
                  pl.BlockSpec((None, K, N), lambda i: (layer, 0, 0))],
        out_specs=pl.BlockSpec((tm, N), lambda i: (i, 0)),
        out_shape=jax.ShapeDtypeStruct((T, N), BF16),
        compiler_params=_params(("parallel",)),
        name="in_proj",
    )(xb, wb)


def _nt_dot(a, b):
    return lax.dot_general(a, b, (((1,), (1,)), ((), ())), preferred_element_type=F32)


def _diff_parts(sc_ref, slope_ref, q_ref, k_ref, v_ref, g_ref, o_ref, hd, qi, blk):
    lam = sc_ref[0]
    post = sc_ref[1]
    slope = slope_ref[hd]

    lane = lax.broadcasted_iota(jnp.int32, (blk, SLOT), 1)
    q = q_ref[...]
    zero = jnp.zeros_like(q)
    qs = jnp.concatenate([jnp.where(lane < HEAD_DIM, q, zero), jnp.where(lane >= HEAD_DIM, q, zero)], axis=0)

    colf = lax.broadcasted_iota(jnp.int32, (1, blk), 1).astype(F32)
    ones = jnp.ones((blk, SLOT), BF16)

    def step(j, carry, masked):
        m_old, a_old = carry
        kb = k_ref[pl.ds(pl.multiple_of(j * blk, blk), blk), :]
        vb = v_ref[pl.ds(pl.multiple_of(j * blk, blk), blk), :]
        vaug = jnp.concatenate([vb, ones], axis=1)
        bias = (slope * LOG2E) * (colf + ((j - qi) * blk).astype(F32))
        s = _nt_dot(qs, kb) + bias
        if masked:
            t = lax.broadcasted_iota(jnp.int32, (2 * blk, blk), 0) & (blk - 1)
            s = jnp.where(lax.broadcasted_iota(jnp.int32, (2 * blk, blk), 1) <= t, s, NEG_INF)
        m_new = jnp.maximum(m_old, jnp.max(s, axis=1, keepdims=True))
        alpha = jnp.exp2(m_old - m_new)
        p = jnp.exp2(s - m_new).astype(BF16)
        return m_new, alpha * a_old + jnp.dot(p, vaug, preferred_element_type=F32)

    def finish(carry):
        _, a = carry
        a1, a2 = a[:blk], a[blk:]
        d = a1[:, :SLOT] / a1[:, SLOT:] - lam * (a2[:, :SLOT] / a2[:, SLOT:])
        d = d * lax.rsqrt(jnp.mean(d * d, axis=1, keepdims=True) + RMS_EPS)
        o_ref[...] = (d * g_ref[...] * post).astype(o_ref.dtype)

    init = (jnp.full((2 * blk, 1), NEG_INF, F32), jnp.zeros((2 * blk, 2 * SLOT), F32))
    return init, step, finish


def _sb_parts(q_ref, k_ref, v_ref, o_ref, qi, tq, tk, n_part):
    nsub = tq // tk
    lane = lax.broadcasted_iota(jnp.int32, (tq, SLOT), 1)
    q = q_ref[...]
    zero = jnp.zeros_like(q)
    qs = jnp.concatenate([jnp.where(lane < HEAD_DIM, q, zero), jnp.where(lane >= HEAD_DIM, q, zero)], axis=0)

    row = lax.broadcasted_iota(jnp.int32, (tk, tk), 0)
    col = lax.broadcasted_iota(jnp.int32, (tk, tk), 1)
    half = jnp.concatenate([jnp.where(row > col, -1.0, 0.0).astype(BF16), jnp.full((tk, tk), -1.0, BF16)], axis=1)
    cum_w = jnp.concatenate([half, half], axis=0)
    sign_bit = jnp.int32(-2 ** 31)

    sub = [slice(s * tk, (s + 1) * tk) for s in range(nsub)]
    rp = 2 * tq // n_part
    parts = [slice(p * rp, (p + 1) * rp) for p in range(n_part)]

    def chunk(c, carry, masked):
        later, acc = carry
        off = pl.multiple_of(c * tq, tq)
        kb = k_ref[pl.ds(off, tq), :]
        vb = v_ref[pl.ds(off, tq), :]
        zs = [_nt_dot(qs[pr], kb) for pr in parts]
        mid = []
        for p, z in enumerate(zs):
            neg_abs = lax.bitcast_convert_type(lax.bitcast_convert_type(z, jnp.int32) | sign_bit, F32)
            sp = jnp.maximum(z, 0.0) + LOG2E * jnp.log(1.0 + jnp.exp2(neg_abs))
            ls = z - sp
            lf = sp
            strict = None
            if masked:
                t = (lax.broadcasted_iota(jnp.int32, z.shape, 0) + p * rp) & (tq - 1)
                strict = lax.broadcasted_iota(jnp.int32, z.shape, 1) < t
                lf = jnp.where(strict, lf, 0.0)
            hi = lf.astype(BF16)
            lo = (lf - hi.astype(F32)).astype(BF16)
            stacked = jnp.concatenate([jnp.concatenate([hi[:, c_], lo[:, c_]], axis=1) for c_ in sub], axis=0)
            cs = jnp.dot(stacked, cum_w, preferred_element_type=F32)
            mid.append((ls, cs, strict))
        later_out, acc_out = [], []
        for pr, (ls, cs, strict) in zip(parts, mid):
            lat = later[pr]
            ws = [None] * nsub
            for s in reversed(range(nsub)):
                cs_s = cs[s * rp:(s + 1) * rp]
                w = jnp.exp2(ls[:, sub[s]] + cs_s[:, :tk] + lat)
                if masked:
                    w = jnp.where(strict[:, sub[s]], w, 0.0)
                ws[s] = w.astype(BF16)
                lat = lat + cs_s[:, tk:]
            later_out.append(lat)
            acc_out.append(acc[pr] + jnp.dot(jnp.concatenate(ws, axis=1), vb, preferred_element_type=F32))
        return jnp.concatenate(later_out, axis=0), jnp.concatenate(acc_out, axis=0)

    def finish(carry):
        _, acc = carry
        o_ref[...] = jnp.where(lane < HEAD_DIM, acc[:tq], acc[tq:]).astype(o_ref.dtype)

    init = (jnp.zeros((2 * tq, SLOT), F32), jnp.zeros((2 * tq, SLOT), F32))
    return init, chunk, finish


def _mixer_kernel(sc_ref, slope_ref, *refs, blk, tk, n_part, n_pair):
    hd = pl.program_id(1)
    qi = pl.program_id(2)
    ins, (od_ref, os_ref) = refs[:7 * n_pair], refs[7 * n_pair:]
    d_parts, s_parts = [], []
    for p in range(n_pair):
        dq, dk, dv, g, sq, sk, sv = ins[7 * p:7 * p + 7]
        cols = slice(p * SLOT, (p + 1) * SLOT)
        d_parts.append(_diff_parts(sc_ref, slope_ref, dq, dk, dv, g, od_ref.at[:, cols], hd * n_pair + p, qi, blk))
        s_parts.append(_sb_parts(sq, sk, sv, os_ref.at[:, cols], qi, blk, tk, n_part))

    s_carry = tuple(sp[1](qi, sp[0], True) for sp in s_parts)
    d_carry = tuple(dp[1](qi, dp[0], True) for dp in d_parts)

    def body(j, carry):
        d_carry, s_carry = carry
        return (tuple(dp[1](j, c, False) for dp, c in zip(d_parts, d_carry)),
                tuple(sp[1](qi - 1 - j, c, False) for sp, c in zip(s_parts, s_carry)))

    d_carry, s_carry = lax.fori_loop(0, qi, body, (d_carry, s_carry))
    for dp, c in zip(d_parts, d_carry):
        dp[2](c)
    for sp, c in zip(s_parts, s_carry):
        sp[2](c)


def _mixer(proj, scalars, slopes, subln_g, *, blk=512, tk=128, n_part=2, n_pair=4):
    B, S, _ = proj.shape
    kblk = ATT_WIDTH // SLOT
    base = 3 * kblk

    def qspec(off, p):
        return pl.BlockSpec((None, blk, SLOT), lambda b, h, i: (b, i, off + n_pair * h + p))

    def kvspec(off, p):
        return pl.BlockSpec((None, S, SLOT), lambda b, h, i: (b, 0, off + n_pair * h + p))

    in_specs = [pl.BlockSpec(memory_space=pltpu.SMEM), pl.BlockSpec(memory_space=pltpu.SMEM)]
    args = [scalars, slopes]
    for p in range(n_pair):
        in_specs += [qspec(0, p), kvspec(kblk, p), kvspec(2 * kblk, p),
                     pl.BlockSpec((1, SLOT), lambda b, h, i: (0, 0)),
                     qspec(base, p), kvspec(base + kblk, p), kvspec(base + 2 * kblk, p)]
        args += [proj, proj, proj, subln_g, proj, proj, proj]
    out_spec = pl.BlockSpec((None, blk, n_pair * SLOT), lambda b, h, i: (b, i, h))
    out = jax.ShapeDtypeStruct((B, S, ATT_WIDTH), BF16)
    return pl.pallas_call(
        functools.partial(_mixer_kernel, blk=blk, tk=tk, n_part=n_part, n_pair=n_pair),
        grid=(B, DIFF_SLOTS // n_pair, S // blk),
        in_specs=in_specs,
        out_specs=[out_spec, out_spec],
        out_shape=[out, out],
        compiler_params=_params(("parallel", "parallel", "parallel")),
        name="mixer",
    )(*args)


def _layer_norm(r, g, b):
    mu = jnp.mean(r, axis=-1, keepdims=True)
    c = r - mu
    var = jnp.mean(c * c, axis=-1, keepdims=True)
    return c * lax.rsqrt(var + LN_EPS) * g + b


def _out_proj_ln_kernel(h_ref, a_ref, b_ref, w_ref, g_ref, beta_ref, o_ref, ob_ref, *, alpha):
    half = a_ref.shape[1]
    y = jnp.dot(a_ref[...], w_ref[:half, :], preferred_element_type=F32)
    y = y + jnp.dot(b_ref[...], w_ref[half:, :], preferred_element_type=F32)
    out = _layer_norm(alpha * h_ref[...] + y, g_ref[...], beta_ref[...])
    o_ref[...] = out
    ob_ref[...] = out.astype(BF16)


def _out_proj_ln(h, a, b, wb, layer, g, beta, *, alpha, tm=512):
    T, D = h.shape
    half = a.shape[1]
    return pl.pallas_call(
        functools.partial(_out_proj_ln_kernel, alpha=alpha),
        grid=(T // tm,),
        in_specs=[pl.BlockSpec((tm, D), lambda i: (i, 0)),
                  pl.BlockSpec((tm, half), lambda i: (i, 0)),
                  pl.BlockSpec((tm, half), lambda i: (i, 0)),
                  pl.BlockSpec((None, 2 * half, D), lambda i: (layer, 0, 0)),
                  pl.BlockSpec((1, D), lambda i: (0, 0)),
                  pl.BlockSpec((1, D), lambda i: (0, 0))],
        out_specs=[pl.BlockSpec((tm, D), lambda i: (i, 0)),
                   pl.BlockSpec((tm, D), lambda i: (i, 0))],
        out_shape=[jax.ShapeDtypeStruct((T, D), F32), jax.ShapeDtypeStruct((T, D), BF16)],
        compiler_params=_params(("parallel",)),
        name="out_proj_ln",
    )(h, a, b, wb, g, beta)


def _merge_sort_network(lo, hi):
    def merge(lo, hi, r):
        step = 2 * r
        if step < hi - lo:
            yield from merge(lo, hi, step)
            yield from merge(lo + r, hi, step)
            yield from ((i, i + r) for i in range(lo + r, hi - r, step))
        else:
            yield (lo, lo + r)
    if hi - lo >= 1:
        mid = lo + (hi - lo) // 2
        yield from _merge_sort_network(lo, mid)
        yield from _merge_sort_network(mid + 1, hi)
        yield from merge(lo, hi, 1)


def _sort_network(n):
    if n & (n - 1) == 0:
        return list(_merge_sort_network(0, n - 1))
    return [(i, i + 1) for rnd in range(n) for i in range(rnd % 2, n - 1, 2)]


def _top16(s):
    n = s.shape[0] // 8
    v = [s[8 * r:8 * (r + 1), :] for r in range(n)]
    for i, j in _sort_network(n):
        v[i], v[j] = jnp.maximum(v[i], v[j]), jnp.minimum(v[i], v[j])
    vals = []
    for k in range(PEER_TOPK):
        m = jnp.max(v[0], axis=0, keepdims=True)
        vals.append(m)
        eq = v[0] == m
        for r in range(min(n, PEER_TOPK - 1 - k)):
            v[r] = jnp.where(eq, v[r + 1] if r + 1 < n else NEG_INF, v[r])
    return jnp.concatenate(vals, axis=0)


def _rank_among(s, v):
    rank = jnp.full(s.shape, float(PEER_TOPK), F32)
    for k in reversed(range(PEER_TOPK)):
        rank = jnp.where(s == v[k:k + 1, :], float(k), rank)
    return rank


def _peer_route_kernel(x_ref, wq_ref, keys_ref, rank2_ref, b_ref, cnt_ref, a_ref):
    tb = x_ref.shape[0]
    qt = _nt_dot(wq_ref[...], x_ref[...]).astype(BF16)
    row8 = lax.broadcasted_iota(jnp.int32, (8, tb), 0)
    row16 = lax.broadcasted_iota(jnp.int32, (PEER_TOPK, tb), 0)
    for h in range(PEER_HEADS):
        s1 = jnp.dot(keys_ref[2 * h], qt[(2 * h) * HALF_Q:(2 * h + 1) * HALF_Q, :], preferred_element_type=F32)
        s2 = jnp.dot(keys_ref[2 * h + 1], qt[(2 * h + 1) * HALF_Q:(2 * h + 2) * HALF_Q, :],
                     preferred_element_type=F32)
        v1 = _top16(s1)
        v2 = _top16(s2)
        rank2 = _rank_among(s2, v2)
        v2lo = v2[0:8, :]
        pieces = [v1 + v2[0:1, :],
                  v1[0:8, :] + v2[1:2, :],
                  jnp.where(row16 >= 2, v1[0:1, :] + v2, NEG_INF),
                  jnp.where(row8 >= 2, v1[1:2, :] + v2lo, NEG_INF),
                  jnp.where((row8 >= 2) & (row8 <= 4), v1[2:3, :] + v2lo, NEG_INF),
                  jnp.where((row8 >= 2) & (row8 <= 3), v1[3:4, :] + v2lo, NEG_INF),
                  jnp.where(row8 == 2, v1[4:5, :] + v2lo, NEG_INF)]
        cand = jnp.concatenate(pieces, axis=0)
        tau = _top16(cand)[PEER_TOPK - 1:PEER_TOPK, :]
        sel = cand >= tau
        top = v1[0:1, :] + v2[0:1, :]
        z = jnp.sum(jnp.where(sel, jnp.exp(cand - top), 0.0), axis=0, keepdims=True)
        self32 = jnp.where(sel, 1.0, 0.0)
        per_row = self32[0:16] + jnp.concatenate([self32[16:24], jnp.zeros((8, tb), F32)], axis=0)
        off = 24
        for k1, rows in enumerate((16, 8, 8, 8, 8)):
            per_row = per_row + jnp.where(row16 == k1, jnp.sum(self32[off:off + rows], axis=0, keepdims=True), 0.0)
            off += rows
        cnt = jnp.zeros(s1.shape, F32)
        for k1 in range(PEER_TOPK):
            cnt = jnp.where(s1 == v1[k1:k1 + 1, :], per_row[k1:k1 + 1, :], cnt)
        rank2_ref[h] = rank2.astype(BF16)
        b_ref[h] = jnp.exp(s2 - v2[0:1, :]).astype(BF16)
        a = jnp.exp(s1 - v1[0:1, :]) * (SQRT_HALF / z)
        for grp in range(tb // LANES):
            cnt_ref[h, grp] = cnt[:, grp * LANES:(grp + 1) * LANES]
            a_ref[h, grp] = a[:, grp * LANES:(grp + 1) * LANES]


def _peer_route(xb, wqt, keys, layer, *, tb=512):
    T, D = xb.shape
    tab_spec = pl.BlockSpec((PEER_HEADS, N_KEYS, tb), lambda i: (0, 0, i))
    return pl.pallas_call(
        _peer_route_kernel,
        grid=(T // tb,),
        in_specs=[pl.BlockSpec((tb, D), lambda i: (i, 0)),
                  pl.BlockSpec((None,) + wqt.shape[1:], lambda i: (layer, 0, 0)),
                  pl.BlockSpec((None,) + keys.shape[1:], lambda i: (layer, 0, 0, 0))],
        out_specs=[tab_spec] * 2 + [pl.BlockSpec((PEER_HEADS, tb // LANES, N_KEYS, LANES), lambda i: (0, i, 0, 0))] * 2,
        out_shape=[jax.ShapeDtypeStruct((PEER_HEADS, N_KEYS, T), BF16)] * 2
        + [jax.ShapeDtypeStruct((PEER_HEADS, T // LANES, N_KEYS, LANES), F32)] * 2,
        compiler_params=_params(("parallel",)),
        name="peer_route",
    )(xb, wqt, keys)


def _peer_expert_kernel(h_ref, x_ref, u_ref, vt_ref, rank2_ref, b_ref, cnt_ref, a_ref, g_ref, beta_ref,
                        o_ref, ob_ref, acc_ref, hid_ref, *, alpha):
    e = pl.program_id(1)

    @pl.when(e == 0)
    def _():
        acc_ref[...] = jnp.zeros_like(acc_ref)

    tb = x_ref.shape[0]
    pack = 16
    zero = jnp.zeros((N_KEYS // pack, pack, tb), BF16)

    def row(ref, h, ii):
        groups = [jnp.broadcast_to(ref[h, grp, ii:ii + 1, :], (pack, LANES)) for grp in range(tb // LANES)]
        return jnp.concatenate(groups, axis=1).astype(BF16)[None]

    act = _nt_dot(u_ref[...], x_ref[...])
    n_i = u_ref.shape[0] // N_KEYS
    per = n_i // 2
    csum = None
    for ii in range(n_i):
        gate = None
        for h in range(PEER_HEADS):
            sel = rank2_ref[h].reshape(zero.shape) < row(cnt_ref, h, ii)
            term = jnp.where(sel, b_ref[h].reshape(zero.shape), zero) * row(a_ref, h, ii)
            gate = term if gate is None else gate + term
        a = act[ii * N_KEYS:(ii + 1) * N_KEYS, :]
        hid = gate.reshape(N_KEYS, tb) * (a * (1.0 + lax.erf(a))).astype(BF16)
        hid_ref[ii * N_KEYS:(ii + 1) * N_KEYS, :] = hid
        if (ii + 1) % per == 0:
            rows = slice((ii + 1 - per) * N_KEYS, (ii + 1) * N_KEYS)
            part = jnp.dot(vt_ref[:, rows], hid_ref[rows, :], preferred_element_type=F32)
            csum = part if csum is None else csum + part
    acc_ref[...] += csum

    @pl.when(e == pl.num_programs(1) - 1)
    def _():
        y = acc_ref[...].T
        out = _layer_norm(alpha * h_ref[...] + y, g_ref[...], beta_ref[...])
        o_ref[...] = out
        ob_ref[...] = out.astype(BF16)


def _peer_expert(h, xb, ub, vtb, layer, tables, g, beta, *, alpha, tb=512, eb=2048):
    T, D = h.shape
    E = ub.shape[1]
    rank2, bexp, cnt, a = tables
    n_i = eb // N_KEYS
    full_tab = pl.BlockSpec((PEER_HEADS, N_KEYS, tb), lambda t, e: (0, 0, t))
    row_tab = pl.BlockSpec((PEER_HEADS, tb // LANES, n_i, LANES), lambda t, e: (0, t, e, 0))
    return pl.pallas_call(
        functools.partial(_peer_expert_kernel, alpha=alpha),
        grid=(T // tb, E // eb),
        in_specs=[pl.BlockSpec((tb, D), lambda t, e: (t, 0)),
                  pl.BlockSpec((tb, D), lambda t, e: (t, 0)),
                  pl.BlockSpec((None, eb, D), lambda t, e: (layer, e, 0)),
                  pl.BlockSpec((None, D, eb), lambda t, e: (layer, 0, e)),
                  full_tab, full_tab, row_tab, row_tab,
                  pl.BlockSpec((1, D), lambda t, e: (0, 0)),
                  pl.BlockSpec((1, D), lambda t, e: (0, 0))],
        out_specs=[pl.BlockSpec((tb, D), lambda t, e: (t, 0)),
                   pl.BlockSpec((tb, D), lambda t, e: (t, 0))],
        out_shape=[jax.ShapeDtypeStruct((T, D), F32), jax.ShapeDtypeStruct((T, D), BF16)],
        scratch_shapes=[pltpu.VMEM((D, tb), F32), pltpu.VMEM((eb, tb), BF16)],
        compiler_params=_params(("parallel", "arbitrary")),
        name="peer_expert",
    )(h, xb, ub, vtb, rank2, bexp, cnt, a, g, beta)


def kernel(x, w_in, lam_q1, lam_k1, lam_q2, lam_k2, subln_g, w_o, ln1_g, ln1_b, w_query, sub_keys, expert_u,
           expert_v, ln2_g, ln2_b):
    B, S, D = x.shape
    T = B * S
    depth = w_in.shape[0]
    alpha = (2.0 * depth) ** 0.25
    slopes = jnp.asarray([2.0 ** (-8.0 * (i + 1) / DIFF_HEADS) for i in range(DIFF_HEADS)], F32)

    w_in_b = w_in.astype(BF16)
    w_o_b = w_o.astype(BF16)
    wq_t = jnp.swapaxes(w_query, 1, 2).astype(BF16)
    keys = sub_keys.reshape(depth, 2 * PEER_HEADS, N_KEYS, HALF_Q).astype(BF16)
    u_b = (expert_u * SQRT_HALF).astype(BF16)
    vt_b = jnp.swapaxes(expert_v, 1, 2).astype(BF16)

    h = x.reshape(T, D)
    hb = h.astype(BF16)
    for l in range(depth):
        lambda_init = 0.8 - 0.6 * math.exp(-0.3 * l)
        lam = jnp.exp(jnp.sum(lam_q1[l] * lam_k1[l])) - jnp.exp(jnp.sum(lam_q2[l] * lam_k2[l])) + lambda_init
        scalars = jnp.stack([lam, jnp.asarray(1.0 - lambda_init, F32)]).astype(F32)

        proj = _in_proj(hb, w_in_b, l).reshape(B, S, -1)
        diff, sb = _mixer(proj, scalars, slopes, subln_g[l].reshape(1, SLOT))
        h, hb = _out_proj_ln(h, diff.reshape(T, ATT_WIDTH), sb.reshape(T, ATT_WIDTH), w_o_b, l,
                             ln1_g[l].reshape(1, D), ln1_b[l].reshape(1, D), alpha=alpha)

        tables = _peer_route(hb, wq_t, keys, l)
        h, hb = _peer_expert(h, hb, u_b, vt_b, l, tables,
                             ln2_g[l].reshape(1, D), ln2_b[l].reshape(1, D), alpha=alpha)
    return h.reshape(B, S, D)
```
